```python
import math
import jax, jax.numpy as jnp
from jax import lax
import numpy as np

D_MODEL = 1024
BATCH = 32
SEQ = 2048
DEPTH = 1

HEAD_DIM = 64
N_HEADS_DIL = 8
N_HEADS_MOBA = 8
N_HEADS_SELF = N_HEADS_DIL + N_HEADS_MOBA
DIL_WIDTH = N_HEADS_DIL * HEAD_DIM
MOBA_WIDTH = N_HEADS_MOBA * HEAD_DIM
MIX_WIDTH = DIL_WIDTH + MOBA_WIDTH
DILATION_PAIRS = ((128, 1), (512, 4), (2048, 16))
MOBA_BLOCK = 256
MOBA_TOPK = 3
MOBA_QCHUNK = 16
N_BUCKETS = 32
MAX_DISTANCE = 2048
N_MEM = 256
N_HEADS_MEM = 4
HEAD_DIM_MEM = D_MODEL // N_HEADS_MEM
D_FF = 2816
CONV_WIDTH = 3
EPS = 1e-6
NEG_INF = -1e30

kernel_name = "hymba_dilated_moba_convffn_block"


def rmsnorm(x, g):
    xf = x.astype(jnp.float32)
    y = xf * lax.rsqrt(jnp.mean(xf * xf, axis=-1, keepdims=True) + EPS)
    return (y * g.astype(jnp.float32)).astype(x.dtype)


def rel_bucket(dist):
    max_exact = N_BUCKETS // 2
    n = jnp.maximum(dist, 0)
    nf = jnp.maximum(n, 1).astype(jnp.float32)
    large = max_exact + (jnp.log(nf / max_exact) / math.log(MAX_DISTANCE / max_exact)
                         * (N_BUCKETS - max_exact)).astype(jnp.int32)
    large = jnp.minimum(large, N_BUCKETS - 1)
    return jnp.where(n < max_exact, n, large)


def dilated_attention(q, k, v, bias_table):
    B, H, S, hd = q.shape
    scale = hd ** -0.5
    outs, lses = [], []
    for window, dil in DILATION_PAIRS:
        w_sub = window // dil
        L = S // dil
        nblk = -(-L // w_sub)
        Lp = nblk * w_sub

        def to_sub(t):
            t = t.reshape(B, H, L, dil, hd).transpose(0, 1, 3, 2, 4)
            t = jnp.pad(t, ((0, 0), (0, 0), (0, 0), (0, Lp - L), (0, 0)))
            return t.reshape(B, H, dil, nblk, w_sub, hd)

        def with_prev(t):
            prev = jnp.pad(t, ((0, 0), (0, 0), (0, 0), (1, 0), (0, 0), (0, 0)))[:, :, :, :-1]
            return jnp.concatenate([prev, t], axis=4)

        qs = to_sub(q)
        kb = with_prev(to_sub(k))
        vb = with_prev(to_sub(v))
        s = jnp.einsum('bhrnqd,bhrnkd->bhrnqk', qs, kb).astype(jnp.float32) * scale
        i = jnp.arange(w_sub)[:, None]
        j = jnp.arange(2 * w_sub)[None, :]
        diff = w_sub + i - j
        bias = bias_table[:, rel_bucket(diff * dil)].astype(jnp.float32)
        blk = jnp.arange(nblk)[:, None, None]
        valid = (diff >= 0) & (diff <= w_sub) & ((blk - 1) * w_sub + j >= 0)
        s = jnp.where(valid, s + bias[:, None, None], NEG_INF)
        m = jnp.max(s, axis=-1, keepdims=True)
        p = jnp.exp(s - m)
        den = jnp.sum(p, axis=-1, keepdims=True)
        o = jnp.einsum('bhrnqk,bhrnkd->bhrnqd', (p / den).astype(v.dtype), vb)
        lse = (m + jnp.log(den))[..., 0]
        o = o.reshape(B, H, dil, Lp, hd)[:, :, :, :L].transpose(0, 1, 3, 2, 4).reshape(B, H, S, hd)
        lse = lse.reshape(B, H, dil, Lp)[:, :, :, :L].transpose(0, 1, 3, 2).reshape(B, H, S)
        outs.append(o)
        lses.append(lse)
    wts = jax.nn.softmax(jnp.stack(lses), axis=0)
    out = jnp.einsum('pbhs,pbhsd->bhsd', wts, jnp.stack(outs).astype(jnp.float32))
    return out.astype(q.dtype)


def moba_attention(q, k, v, bias_table):
    B, H, S, hd = q.shape
    scale = hd ** -0.5
    nb = -(-S // MOBA_BLOCK)
    Sp = nb * MOBA_BLOCK
    pad = lambda t: jnp.pad(t, ((0, 0), (0, 0), (0, Sp - S), (0, 0)))
    qp, kp, vp = pad(q), pad(k), pad(v)
    kb = kp.reshape(B, H, nb, MOBA_BLOCK, hd)
    vb = vp.reshape(B, H, nb, MOBA_BLOCK, hd)
    k_mean = jnp.mean(kb.astype(jnp.float32), axis=3).astype(q.dtype)
    topk = min(MOBA_TOPK, nb)
    n_chunks = Sp // MOBA_QCHUNK
    qc = qp.reshape(B, H, n_chunks, MOBA_QCHUNK, hd).transpose(2, 0, 1, 3, 4)
    bi = jnp.arange(B)[:, None, None, None]
    hi = jnp.arange(H)[None, :, None, None]
    hi5 = jnp.arange(H)[None, :, None, None, None]
    bias_f = bias_table.astype(jnp.float32)

    def one_chunk(args):
        q_c, c = args
        t = c * MOBA_QCHUNK + jnp.arange(MOBA_QCHUNK)
        own = (c * MOBA_QCHUNK) // MOBA_BLOCK
        gate = jnp.einsum('bhqd,bhnd->bhqn', q_c, k_mean).astype(jnp.float32)
        gate = jnp.where(jnp.arange(nb) < own, gate, NEG_INF)
        _, idx = lax.top_k(gate, topk)
        sel_valid = jnp.arange(topk) < own
        k_sel = kb[bi, hi, idx]
        v_sel = vb[bi, hi, idx]
        s_sel = jnp.einsum('bhqd,bhqnkd->bhqnk', q_c, k_sel).astype(jnp.float32) * scale
        key_pos = idx[..., None] * MOBA_BLOCK + jnp.arange(MOBA_BLOCK)
        dist = t[:, None, None] - key_pos
        s_sel = jnp.where(sel_valid[:, None], s_sel + bias_f[hi5, rel_bucket(dist)], NEG_INF)
        k_own = lax.dynamic_slice_in_dim(kp, own * MOBA_BLOCK, MOBA_BLOCK, axis=2)
        v_own = lax.dynamic_slice_in_dim(vp, own * MOBA_BLOCK, MOBA_BLOCK, axis=2)
        s_own = jnp.einsum('bhqd,bhkd->bhqk', q_c, k_own).astype(jnp.float32) * scale
        dist_own = t[:, None] - (own * MOBA_BLOCK + jnp.arange(MOBA_BLOCK))[None, :]
        s_own = jnp.where(dist_own >= 0, s_own + bias_f[:, rel_bucket(dist_own)][None], NEG_INF)
        nsel = topk * MOBA_BLOCK
        s_all = jnp.concatenate([s_sel.reshape(B, H, MOBA_QCHUNK, nsel), s_own], axis=-1)
        p = jax.nn.softmax(s_all, axis=-1).astype(v.dtype)
        p_sel = p[..., :nsel].reshape(B, H, MOBA_QCHUNK, topk, MOBA_BLOCK)
        o = (jnp.einsum('bhqnk,bhqnkd->bhqd', p_sel, v_sel)
             + jnp.einsum('bhqk,bhkd->bhqd', p[..., nsel:], v_own))
        return o

    out = lax.map(one_chunk, (qc, jnp.arange(n_chunks)))
    return out.transpose(1, 2, 0, 3, 4).reshape(B, H, Sp, hd)[:, :, :S]


def memory_cross_attention(c, m, w_q, w_kv, w_o):
    B, S, D = c.shape
    qm = (c @ w_q).reshape(B, S, N_HEADS_MEM, HEAD_DIM_MEM)
    km, vm = jnp.split(m @ w_kv, 2, axis=-1)
    km = km.reshape(B, N_MEM, N_HEADS_MEM, HEAD_DIM_MEM)
    vm = vm.reshape(B, N_MEM, N_HEADS_MEM, HEAD_DIM_MEM)
    s = jnp.einsum('bqhd,bkhd->bhqk', qm, km).astype(jnp.float32) * (HEAD_DIM_MEM ** -0.5)
    p = jax.nn.softmax(s, axis=-1).astype(c.dtype)
    o = jnp.einsum('bhqk,bkhd->bqhd', p, vm).reshape(B, S, D)
    return o @ w_o


def conv_ffn(f, w_gate, w_up, conv_w, conv_b, w_down):
    S = f.shape[1]
    a = f @ w_gate
    ap = jnp.pad(a, ((0, 0), (CONV_WIDTH - 1, 0), (0, 0)))
    conv = conv_b
    for tap in range(CONV_WIDTH):
        conv = conv + ap[:, tap:tap + S] * conv_w[tap]
    return (jax.nn.silu(conv) * (f @ w_up)) @ w_down


def setup_inputs(seed: int = 0) -> dict:
    key = jax.random.key(seed)
    ks = jax.random.split(key, 20)
    f32 = jnp.float32
    nrm = lambda k, shape, scale: jax.random.normal(k, shape, f32) * scale
    gain = lambda k, shape: 1.0 + 0.02 * jax.random.normal(k, shape, f32)
    return {
        "x": nrm(ks[0], (BATCH, SEQ, D_MODEL), 1.0),
        "mem": nrm(ks[1], (BATCH, N_MEM, D_MODEL), 1.0),
        "w_in": nrm(ks[2], (DEPTH, D_MODEL, 3 * MIX_WIDTH), D_MODEL ** -0.5),
        "g_mix": gain(ks[3], (DEPTH, D_MODEL)),
        "g_out_dil": gain(ks[4], (DEPTH, DIL_WIDTH)),
        "g_out_moba": gain(ks[5], (DEPTH, MOBA_WIDTH)),
        "w_out": nrm(ks[6], (DEPTH, MIX_WIDTH, D_MODEL), MIX_WIDTH ** -0.5),
        "rel_bias": nrm(ks[7], (N_HEADS_SELF, N_BUCKETS), 0.5),
        "g_cross": gain(ks[8], (DEPTH, D_MODEL)),
        "g_mem": gain(ks[9], (DEPTH, D_MODEL)),
        "w_q_mem": nrm(ks[10], (DEPTH, D_MODEL, D_MODEL), D_MODEL ** -0.5),
        "w_kv_mem": nrm(ks[11], (DEPTH, D_MODEL, 2 * D_MODEL), D_MODEL ** -0.5),
        "w_o_mem": nrm(ks[12], (DEPTH, D_MODEL, D_MODEL), D_MODEL ** -0.5),
        "g_ffn": gain(ks[13], (DEPTH, D_MODEL)),
        "w_gate": nrm(ks[14], (DEPTH, D_MODEL, D_FF), D_MODEL ** -0.5),
        "w_up": nrm(ks[15], (DEPTH, D_MODEL, D_FF), D_MODEL ** -0.5),
        "conv_w": nrm(ks[16], (DEPTH, CONV_WIDTH, D_FF), CONV_WIDTH ** -0.5),
        "conv_b": nrm(ks[17], (DEPTH, D_FF), 0.02),
        "w_down": nrm(ks[18], (DEPTH, D_FF, D_MODEL), D_FF ** -0.5),
        "g_final": gain(ks[19], (D_MODEL,)),
    }


def reference(x, mem, w_in, g_mix, g_out_dil, g_out_moba, w_out, rel_bias, g_cross, g_mem,
              w_q_mem, w_kv_mem, w_o_mem, g_ffn, w_gate, w_up, conv_w, conv_b, w_down, g_final):
    B, S, _ = x.shape
    heads = lambda t, n: t.reshape(B, S, n, HEAD_DIM).transpose(0, 2, 1, 3)
    merge = lambda t, w: t.transpose(0, 2, 1, 3).reshape(B, S, w)
    h = x
    for l in range(DEPTH):
        u = rmsnorm(h, g_mix[l])
        qa, ka, va, qb, kb, vb = jnp.split(u @ w_in[l], 6, axis=-1)
        ya = dilated_attention(heads(qa, N_HEADS_DIL), heads(ka, N_HEADS_DIL),
                               heads(va, N_HEADS_DIL), rel_bias[:N_HEADS_DIL])
        yb = moba_attention(heads(qb, N_HEADS_MOBA), heads(kb, N_HEADS_MOBA),
                            heads(vb, N_HEADS_MOBA), rel_bias[N_HEADS_DIL:])
        y = jnp.concatenate([rmsnorm(merge(ya, DIL_WIDTH), g_out_dil[l]),
                             rmsnorm(merge(yb, MOBA_WIDTH), g_out_moba[l])], axis=-1)
        h = h + y @ w_out[l]
        h = h + memory_cross_attention(rmsnorm(h, g_cross[l]), rmsnorm(mem, g_mem[l]),
                                       w_q_mem[l], w_kv_mem[l], w_o_mem[l])
        h = h + conv_ffn(rmsnorm(h, g_ffn[l]), w_gate[l], w_up[l], conv_w[l], conv_b[l], w_down[l])
    return rmsnorm(h, g_final)
```

```python
import functools
import math

import jax
import jax.numpy as jnp
from jax import lax
from jax.experimental import pallas as pl
from jax.experimental.pallas import tpu as pltpu

F32 = jnp.float32
BF16 = jnp.bfloat16

HEAD_DIM = 64
LANES = 128
HEADS_PER_SLAB = LANES // HEAD_DIM
N_HEADS_DIL = 8
N_HEADS_MOBA = 8
DILATION_PAIRS = ((128, 1), (512, 4), (2048, 16))
W_SUB = 128
MOBA_BLOCK = 256
MOBA_TOPK = 3
N_BUCKETS = 32
MAX_DISTANCE = 2048
N_HEADS_MEM = 4
CONV_WIDTH = 3
EPS = 1e-6
NEG_INF = -1e30

ROW_TILE = 512
FFN_CHUNK = 256
HALO_ROWS = 16
VMEM_LIMIT = 56 * 1024 * 1024


def _rms(x, g):
    return x * lax.rsqrt(jnp.mean(x * x, axis=-1, keepdims=True) + EPS) * g


def _nt_dot(a, b):
    return lax.dot_general(a, b, (((1,), (1,)), ((), ())), preferred_element_type=F32)


def _dot(a, b):
    return jnp.dot(a, b, preferred_element_type=F32)


def _rel_bucket(dist):
    max_exact = N_BUCKETS // 2
    n = jnp.maximum(dist, 0)
    nf = jnp.maximum(n, 1).astype(jnp.float32)
    large = max_exact + (jnp.log(nf / max_exact) / math.log(MAX_DISTANCE / max_exact)
                         * (N_BUCKETS - max_exact)).astype(jnp.int32)
    large = jnp.minimum(large, N_BUCKETS - 1)
    return jnp.where(n < max_exact, n, large)


def _inproj_kernel(x_ref, g_ref, w_ref, o_ref):
    u = _rms(x_ref[0], g_ref[...]).astype(BF16)
    group = o_ref.shape[2] * LANES
    for c in range(o_ref.shape[0]):
        res = _dot(u, w_ref[:, c * group:(c + 1) * group])
        for hp in range(o_ref.shape[2]):
            o_ref[c, 0, hp] = res[:, hp * LANES:(hp + 1) * LANES].astype(BF16)


def _inproj(x, g, w):
    B, S, D = x.shape
    n_groups = w.shape[1] // (4 * LANES)
    return pl.pallas_call(
        _inproj_kernel,
        grid=(B, S // ROW_TILE),
        in_specs=[
            pl.BlockSpec((1, ROW_TILE, D), lambda b, t: (b, t, 0)),
            pl.BlockSpec((1, D), lambda b, t: (0, 0)),
            pl.BlockSpec(w.shape, lambda b, t: (0, 0)),
        ],
        out_specs=pl.BlockSpec((n_groups, 1, 4, ROW_TILE, LANES), lambda b, t: (0, b, 0, t, 0)),
        out_shape=jax.ShapeDtypeStruct((n_groups, B, 4, S, LANES), BF16),
        compiler_params=pltpu.CompilerParams(vmem_limit_bytes=VMEM_LIMIT),
        name="inproj",
    )(x, g, w)


def _pair_tile(qt, kt, vt, bias2, lane_lo):
    parts = []
    for h in range(HEADS_PER_SLAB):
        in_head = lane_lo if h == 0 else jnp.logical_not(lane_lo)
        qh = jnp.where(in_head, qt, jnp.zeros_like(qt))
        s = _nt_dot(qh, kt) + bias2[h]
        m = jnp.max(s, axis=-1, keepdims=True)
        p = jnp.exp(s - m)
        l = jnp.sum(p, axis=-1, keepdims=True)
        acc = _dot(p.astype(BF16), vt)
        parts.append((m, l, acc))
    (m0, l0, a0), (m1, l1, a1) = parts
    return (jnp.where(lane_lo, m0, m1), jnp.where(lane_lo, l0, l1), jnp.where(lane_lo, a0, a1))


def _dilated_kernel(q_ref, k_ref, v_ref, bias_ref, o_ref, qf, kf, vf, m_s, l_s, a_s):
    S = q_ref.shape[0]
    lane_lo = lax.broadcasted_iota(jnp.int32, (1, LANES), 1) < HEAD_DIM

    def bias_own(p):
        return tuple(bias_ref[p, h, :, W_SUB:] for h in range(HEADS_PER_SLAB))

    def bias_full(p):
        return tuple(bias_ref[p, h] for h in range(HEADS_PER_SLAB))

    mt, lt, at = _pair_tile(q_ref[0:W_SUB], k_ref[0:W_SUB], v_ref[0:W_SUB], bias_own(0), lane_lo)
    m_s[0:W_SUB] = mt
    l_s[0:W_SUB] = lt
    a_s[0:W_SUB] = at

    def body_d1(n, carry):
        r0 = pl.multiple_of(n * W_SUB, W_SUB)
        rp = pl.multiple_of((n - 1) * W_SUB, W_SUB)
        mt, lt, at = _pair_tile(q_ref[pl.ds(r0, W_SUB)], k_ref[pl.ds(rp, 2 * W_SUB)],
                                v_ref[pl.ds(rp, 2 * W_SUB)], bias_full(0), lane_lo)
        m_s[pl.ds(r0, W_SUB)] = mt
        l_s[pl.ds(r0, W_SUB)] = lt
        a_s[pl.ds(r0, W_SUB)] = at
        return carry

    lax.fori_loop(1, S // W_SUB, body_d1, 0)

    qf[...] = q_ref[...].astype(F32)
    kf[...] = k_ref[...].astype(F32)
    vf[...] = v_ref[...].astype(F32)

    def merge(rows, mt, lt, at):
        m_old = m_s[rows, :]
        m_new = jnp.maximum(m_old, mt)
        a = jnp.exp(m_old - m_new)
        b = jnp.exp(mt - m_new)
        m_s[rows, :] = m_new
        l_s[rows, :] = a * l_s[rows, :] + b * lt
        a_s[rows, :] = a * a_s[rows, :] + b * at

    def body_res(r, carry, p, d):
        L = S // d
        qs = qf[pl.ds(r, L, stride=d), :].astype(BF16)
        ks = kf[pl.ds(r, L, stride=d), :].astype(BF16)
        vs = vf[pl.ds(r, L, stride=d), :].astype(BF16)
        for n in range(L // W_SUB):
            qt = qs[n * W_SUB:(n + 1) * W_SUB]
            if n == 0:
                mt, lt, at = _pair_tile(qt, ks[0:W_SUB], vs[0:W_SUB], bias_own(p), lane_lo)
            else:
                mt, lt, at = _pair_tile(qt, ks[(n - 1) * W_SUB:(n + 1) * W_SUB],
                                        vs[(n - 1) * W_SUB:(n + 1) * W_SUB], bias_full(p), lane_lo)
            merge(pl.ds(n * W_SUB * d + r, W_SUB, stride=d), mt, lt, at)
        return carry

    for p in range(1, len(DILATION_PAIRS)):
        d = DILATION_PAIRS[p][1]
        lax.fori_loop(0, d, functools.partial(body_res, p=p, d=d), 0)

    o_ref[...] = (a_s[...] / l_s[...]).astype(o_ref.dtype)


def _dilated_attention(qkv, bias):
    _, B, HP, S, _ = qkv.shape
    qkv_spec = lambda g: pl.BlockSpec((None, None, None, S, LANES), lambda hp, b, g=g: (g, b, hp, 0, 0))
    return pl.pallas_call(
        _dilated_kernel,
        grid=(HP, B),
        in_specs=[qkv_spec(0), qkv_spec(1), qkv_spec(2),
                  pl.BlockSpec((None,) + bias.shape[1:], lambda hp, b: (hp, 0, 0, 0, 0))],
        out_specs=pl.BlockSpec((None, None, S, LANES), lambda hp, b: (b, hp, 0, 0)),
        out_shape=jax.ShapeDtypeStruct((B, HP, S, LANES), F32),
        scratch_shapes=[pltpu.VMEM((S, LANES), F32)] * 6,
        compiler_params=pltpu.CompilerParams(vmem_limit_bytes=VMEM_LIMIT),
        name="dilated_attn",
    )(qkv, qkv, qkv, bias)


def _moba_kernel(q_ref, k_ref, v_ref, bias_ref, o_ref, qa_s, ka_s):
    S = q_ref.shape[0]
    nb = S // MOBA_BLOCK
    n_cmp = nb * nb
    lane = lax.broadcasted_iota(jnp.int32, (1, LANES), 1)
    lane_lo = lane < HEAD_DIM
    row_blk = lax.shift_right_logical(lax.broadcasted_iota(jnp.int32, (S, LANES), 0),
                                      int(math.log2(MOBA_BLOCK)))
    q = q_ref[...]
    k = k_ref[...]

    kmean = jnp.sum(k.astype(F32).reshape(nb, MOBA_BLOCK, LANES), axis=1) * (1.0 / MOBA_BLOCK)
    pad = jnp.zeros((LANES - n_cmp, LANES), F32)
    rhs_a = jnp.concatenate([kmean] * nb + [pad], axis=0).astype(BF16)
    rhs_b = jnp.concatenate([jnp.broadcast_to(kmean[i:i + 1], (nb, LANES)) for i in range(nb)] + [pad],
                            axis=0).astype(BF16)
    ci = lax.shift_right_logical(lane, int(math.log2(nb)))
    cj = lane & (nb - 1)
    rr = lax.broadcasted_iota(jnp.int32, (LANES, LANES), 0)
    cc = lax.broadcasted_iota(jnp.int32, (LANES, LANES), 1)

    for h in range(HEADS_PER_SLAB):
        in_head = lane_lo if h == 0 else jnp.logical_not(lane_lo)
        pen_base = HEAD_DIM * (1 - h)
        zero = jnp.zeros((), BF16)
        ga = _nt_dot(q, jnp.where(in_head, rhs_a, zero))
        gb = _nt_dot(q, jnp.where(in_head, rhs_b, zero))
        beats = ((gb > ga) | ((gb == ga) & (ci < cj))) & (ci < row_blk) & (lane < n_cmp)
        gather = jnp.where(((rr & (nb - 1)) == (cc - pen_base)) & (rr < n_cmp), 1.0, 0.0).astype(BF16)
        rank = _dot(jnp.where(beats, 1.0, 0.0).astype(BF16), gather)
        jl = lane - pen_base
        selected = (jl == row_blk) | ((rank < MOBA_TOPK - 0.5) & (jl < row_blk))
        pen = jnp.where((jl >= 0) & (jl < nb) & jnp.logical_not(selected), NEG_INF, 0.0).astype(BF16)
        qa_s[h] = jnp.where(in_head, q, pen)
        ka_s[h] = jnp.where(in_head, k, jnp.where(jl == row_blk, 1.0, 0.0).astype(BF16))

    def q_block(i, carry):
        r0 = pl.multiple_of(i * MOBA_BLOCK, MOBA_BLOCK)
        outs = []
        for h in range(HEADS_PER_SLAB):
            qi = qa_s[h, pl.ds(r0, MOBA_BLOCK), :]
            s = _nt_dot(qi, ka_s[h, pl.ds(r0, MOBA_BLOCK), :]) + bias_ref[h, 0]
            m = jnp.max(s, axis=-1, keepdims=True)
            p = jnp.exp(s - m)
            l = jnp.sum(p, axis=-1, keepdims=True)
            acc = _dot(p.astype(BF16), v_ref[pl.ds(r0, MOBA_BLOCK), :])

            def kv_step(j, state, h=h, qi=qi):
                m, l, acc = state
                c0 = pl.multiple_of(j * MOBA_BLOCK, MOBA_BLOCK)
                s = _nt_dot(qi, ka_s[h, pl.ds(c0, MOBA_BLOCK), :]) + bias_ref[h, i - j]
                m_new = jnp.maximum(m, jnp.max(s, axis=-1, keepdims=True))
                alpha = jnp.exp(m - m_new)
                p = jnp.exp(s - m_new)
                l = alpha * l + jnp.sum(p, axis=-1, keepdims=True)
                acc = alpha * acc + _dot(p.astype(BF16), v_ref[pl.ds(c0, MOBA_BLOCK), :])
                return m_new, l, acc

            m, l, acc = lax.fori_loop(0, i, kv_step, (m, l, acc))
            outs.append(acc / l)
        o_ref[pl.ds(r0, MOBA_BLOCK), :] = jnp.where(lane_lo, outs[0], outs[1]).astype(o_ref.dtype)
        return carry

    lax.fori_loop(0, nb, q_block, 0)


def _moba_attention(qkv, bias):
    _, B, HP, S, _ = qkv.shape
    qkv_spec = lambda g: pl.BlockSpec((None, None, None, S, LANES), lambda hp, b, g=g: (g, b, hp, 0, 0))
    return pl.pallas_call(
        _moba_kernel,
        grid=(HP, B),
        in_specs=[qkv_spec(3), qkv_spec(4), qkv_spec(5),
                  pl.BlockSpec((None,) + bias.shape[1:], lambda hp, b: (hp, 0, 0, 0, 0))],
        out_specs=pl.BlockSpec((None, None, S, LANES), lambda hp, b: (b, hp, 0, 0)),
        out_shape=jax.ShapeDtypeStruct((B, HP, S, LANES), F32),
        scratch_shapes=[pltpu.VMEM((HEADS_PER_SLAB, S, LANES), BF16)] * 2,
        compiler_params=pltpu.CompilerParams(vmem_limit_bytes=VMEM_LIMIT),
        name="moba_attn",
    )(qkv, qkv, qkv, bias)


def _outproj_kernel(ya_ref, yb_ref, x_ref, ga_ref, gb_ref, wout_ref, gc_ref, wq_ref, h_ref, qm_ref):
    ya = jnp.concatenate([ya_ref[0, hp] for hp in range(ya_ref.shape[1])], axis=-1)
    yb = jnp.concatenate([yb_ref[0, hp] for hp in range(yb_ref.shape[1])], axis=-1)
    y = jnp.concatenate([_rms(ya, ga_ref[...]), _rms(yb, gb_ref[...])], axis=-1).astype(BF16)
    h1 = x_ref[0] + _dot(y, wout_ref[...])
    h_ref[0] = h1
    c = _rms(h1, gc_ref[...]).astype(BF16)
    qm_ref[0] = _dot(c, wq_ref[...]).astype(BF16)


def _outproj(ya, yb, x, g_a, g_b, w_out, g_cross, w_q):
    B, S, D = x.shape
    HP = ya.shape[1]
    const = lambda shape: pl.BlockSpec(shape, lambda b, t: (0,) * len(shape))
    y_spec = pl.BlockSpec((1, HP, ROW_TILE, LANES), lambda b, t: (b, 0, t, 0))
    row_spec = pl.BlockSpec((1, ROW_TILE, D), lambda b, t: (b, t, 0))
    return pl.pallas_call(
        _outproj_kernel,
        grid=(B, S // ROW_TILE),
        in_specs=[y_spec, y_spec, row_spec, const(g_a.shape), const(g_b.shape), const(w_out.shape),
                  const(g_cross.shape), const(w_q.shape)],
        out_specs=[row_spec, row_spec],
        out_shape=[jax.ShapeDtypeStruct((B, S, D), F32), jax.ShapeDtypeStruct((B, S, D), BF16)],
        compiler_params=pltpu.CompilerParams(vmem_limit_bytes=VMEM_LIMIT),
        name="outproj",
    )(ya, yb, x, g_a, g_b, w_out, g_cross, w_q)


def _memkv_kernel(mem_ref, g_ref, w_ref, k_ref, v_ref):
    m = _rms(mem_ref[0], g_ref[...]).astype(BF16)
    kv = _dot(m, w_ref[...])
    d = k_ref.shape[-1]
    k_ref[0] = kv[:, :d].astype(BF16)
    v_ref[0] = kv[:, d:].astype(BF16)


def _memkv(mem, g, w_kv):
    B, M, D = mem.shape
    spec = pl.BlockSpec((1, M, D), lambda b: (b, 0, 0))
    return pl.pallas_call(
        _memkv_kernel,
        grid=(B,),
        in_specs=[spec, pl.BlockSpec(g.shape, lambda b: (0, 0)), pl.BlockSpec(w_kv.shape, lambda b: (0, 0))],
        out_specs=[spec, spec],
        out_shape=[jax.ShapeDtypeStruct((B, M, D), BF16)] * 2,
        compiler_params=pltpu.CompilerParams(vmem_limit_bytes=VMEM_LIMIT),
        name="memkv",
    )(mem, g, w_kv)


def _cross_kernel(qm_ref, km_ref, vm_ref, h_ref, wo_ref, o_ref):
    d_head = qm_ref.shape[-1] // N_HEADS_MEM
    outs = []
    for hd in range(N_HEADS_MEM):
        cols = slice(hd * d_head, (hd + 1) * d_head)
        s = _nt_dot(qm_ref[0, :, cols], km_ref[0, :, cols])
        m = jnp.max(s, axis=-1, keepdims=True)
        p = jnp.exp(s - m)
        l = jnp.sum(p, axis=-1, keepdims=True)
        outs.append((_dot(p.astype(BF16), vm_ref[0, :, cols]) / l).astype(BF16))
    o = jnp.concatenate(outs, axis=-1)
    o_ref[0] = h_ref[0] + _dot(o, wo_ref[...])


def _cross_attention(qm, km, vm, h1, w_o):
    B, S, D = h1.shape
    M = km.shape[1]
    row_spec = pl.BlockSpec((1, ROW_TILE, D), lambda b, t: (b, t, 0))
    mem_spec = pl.BlockSpec((1, M, D), lambda b, t: (b, 0, 0))
    return pl.pallas_call(
        _cross_kernel,
        grid=(B, S // ROW_TILE),
        in_specs=[row_spec, mem_spec, mem_spec, row_spec, pl.BlockSpec(w_o.shape, lambda b, t: (0, 0))],
        out_specs=row_spec,
        out_shape=jax.ShapeDtypeStruct((B, S, D), F32),
        compiler_params=pltpu.CompilerParams(vmem_limit_bytes=VMEM_LIMIT),
        name="cross_attn",
    )(qm, km, vm, h1, w_o)


def _ffn_kernel(h_ref, halo_ref, g_ref, wg_ref, wu_ref, cw_ref, cb_ref, wd_ref, gf_ref, o_ref, *, final_norm):
    tm = h_ref.shape[1]
    h2 = h_ref[0]
    halo = jnp.where(pl.program_id(1) > 0, halo_ref[0], 0.0)
    f = _rms(jnp.concatenate([halo, h2], axis=0), g_ref[...]).astype(BF16)
    acc = jnp.zeros(h2.shape, F32)
    d_ff = wg_ref.shape[1]
    for c in range(d_ff // FFN_CHUNK):
        cols = slice(c * FFN_CHUNK, (c + 1) * FFN_CHUNK)
        a = _dot(f, wg_ref[:, cols])
        a_m1 = pltpu.roll(a, 1, 0)[HALO_ROWS:]
        a_m2 = pltpu.roll(a, 2, 0)[HALO_ROWS:]
        conv = cb_ref[:, cols] + a_m2 * cw_ref[0:1, cols]
        conv = conv + a_m1 * cw_ref[1:2, cols]
        conv = conv + a[HALO_ROWS:] * cw_ref[2:3, cols]
        up = _dot(f[HALO_ROWS:], wu_ref[:, cols])
        gated = (conv * jax.nn.sigmoid(conv) * up).astype(BF16)
        acc = acc + _dot(gated, wd_ref[cols, :])
    out = h2 + acc
    if final_norm:
        out = _rms(out, gf_ref[...])
    o_ref[0] = out


def _ffn(h2, g_ffn, w_gate, w_up, conv_w, conv_b, w_down, g_final, final_norm):
    B, S, D = h2.shape
    assert HALO_ROWS >= CONV_WIDTH - 1 and CONV_WIDTH == 3
    const = lambda a: pl.BlockSpec(a.shape, lambda b, t: (0,) * a.ndim, pipeline_mode=pl.Buffered(1))
    row_spec = pl.BlockSpec((1, ROW_TILE, D), lambda b, t: (b, t, 0))
    halo_spec = pl.BlockSpec((1, HALO_ROWS, D),
                             lambda b, t: (b, jnp.maximum(t * (ROW_TILE // HALO_ROWS) - 1, 0), 0))
    return pl.pallas_call(
        functools.partial(_ffn_kernel, final_norm=final_norm),
        grid=(B, S // ROW_TILE),
        in_specs=[row_spec, halo_spec, const(g_ffn), const(w_gate), const(w_up), const(conv_w),
                  const(conv_b), const(w_down), const(g_final)],
        out_specs=row_spec,
        out_shape=jax.ShapeDtypeStruct((B, S, D), F32),
        compiler_params=pltpu.CompilerParams(vmem_limit_bytes=VMEM_LIMIT),
        name="conv_ffn",
    )(h2, h2, g_ffn, w_gate, w_up, conv_w, conv_b, w_down, g_final)


def _dilated_bias_tiles(table):
    i = jnp.arange(W_SUB)[:, None]
    j = jnp.arange(2 * W_SUB)[None, :]
    diff = W_SUB + i - j
    valid = (diff >= 0) & (diff <= W_SUB)
    tiles = [jnp.where(valid, table[:, _rel_bucket(diff * d)].astype(F32), NEG_INF) for _, d in DILATION_PAIRS]
    t = jnp.stack(tiles, axis=0)
    t = t.reshape(len(DILATION_PAIRS), N_HEADS_DIL // HEADS_PER_SLAB, HEADS_PER_SLAB, W_SUB, 2 * W_SUB)
    return t.transpose(1, 0, 2, 3, 4)


def _moba_bias_tiles(table, nb):
    a = jnp.arange(MOBA_BLOCK)[:, None]
    c = jnp.arange(MOBA_BLOCK)[None, :]
    dist = MOBA_BLOCK * jnp.arange(nb)[:, None, None] + a - c
    t = jnp.where(dist >= 0, table[:, _rel_bucket(dist)].astype(F32), NEG_INF)
    return t.reshape(N_HEADS_MOBA // HEADS_PER_SLAB, HEADS_PER_SLAB, nb, MOBA_BLOCK, MOBA_BLOCK)


def kernel(x, mem, w_in, g_mix, g_out_dil, g_out_moba, w_out, rel_bias, g_cross, g_mem, w_q_mem, w_kv_mem,
           w_o_mem, g_ffn, w_gate, w_up, conv_w, conv_b, w_down, g_final):
    B, S, D = x.shape
    depth = w_in.shape[0]
    dil_width = N_HEADS_DIL * HEAD_DIM
    mix_width = dil_width + N_HEADS_MOBA * HEAD_DIM
    assert S % (W_SUB * DILATION_PAIRS[-1][1]) == 0 and S % MOBA_BLOCK == 0 and S % ROW_TILE == 0
    assert all(w // d == W_SUB for w, d in DILATION_PAIRS) and w_in.shape[2] == 3 * mix_width

    dil_bias = _dilated_bias_tiles(rel_bias[:N_HEADS_DIL])
    moba_bias = _moba_bias_tiles(rel_bias[N_HEADS_DIL:], S // MOBA_BLOCK)
    col = jnp.arange(3 * mix_width)
    is_q = (col < dil_width) | ((col >= 3 * dil_width) & (col < 3 * dil_width + mix_width - dil_width))
    in_scale = jnp.where(is_q, HEAD_DIM ** -0.5, 1.0).astype(F32)
    row2 = lambda v: v.reshape(1, -1)

    h = x
    for l in range(depth):
        qkv = _inproj(h, row2(g_mix[l]), (w_in[l] * in_scale).astype(BF16))
        ya = _dilated_attention(qkv, dil_bias)
        yb = _moba_attention(qkv, moba_bias)
        w_q = (w_q_mem[l] * ((D // N_HEADS_MEM) ** -0.5)).astype(BF16)
        h1, qm = _outproj(ya, yb, h, row2(g_out_dil[l]), row2(g_out_moba[l]), w_out[l].astype(BF16),
                          row2(g_cross[l]), w_q)
        km, vm = _memkv(mem, row2(g_mem[l]), w_kv_mem[l].astype(BF16))
        h2 = _cross_attention(qm, km, vm, h1, w_o_mem[l].astype(BF16))
        h = _ffn(h2, row2(g_ffn[l]), w_gate[l].astype(BF16), w_up[l].astype(BF16), conv_w[l], row2(conv_b[l]),
                 w_down[l].astype(BF16), row2(g_final), final_norm=(l == depth - 1))
    return h
```

```python
import functools
import math

import jax
import jax.numpy as jnp
from jax import lax
from jax.experimental import pallas as pl
from jax.experimental.pallas import tpu as pltpu

F32 = jnp.float32
BF16 = jnp.bfloat16

HEAD_DIM = 64
LANES = 128
HEADS_PER_SLAB = LANES // HEAD_DIM
N_HEADS_DIL = 8
N_HEADS_MOBA = 8
DILATION_PAIRS = ((128, 1), (512, 4), (2048, 16))
W_SUB = 128
MOBA_BLOCK = 256
MOBA_TOPK = 3
N_BUCKETS = 32
MAX_DISTANCE = 2048
N_HEADS_MEM = 4
CONV_WIDTH = 3
EPS = 1e-6
NEG_INF = -1e30

ROW_TILE = 512
FFN_CHUNK = 256
SCORE_LOOKAHEAD = 2
HALO_ROWS = 16
VMEM_LIMIT = 56 * 1024 * 1024


def _rms(x, g):
    return x * lax.rsqrt(jnp.mean(x * x, axis=-1, keepdims=True) + EPS) * g


def _nt_dot(a, b):
    return lax.dot_general(a, b, (((1,), (1,)), ((), ())), preferred_element_type=F32)


def _dot(a, b):
    return jnp.dot(a, b, preferred_element_type=F32)


def _rel_bucket(dist):
    max_exact = N_BUCKETS // 2
    n = jnp.maximum(dist, 0)
    nf = jnp.maximum(n, 1).astype(jnp.float32)
    large = max_exact + (jnp.log(nf / max_exact) / math.log(MAX_DISTANCE / max_exact)
                         * (N_BUCKETS - max_exact)).astype(jnp.int32)
    large = jnp.minimum(large, N_BUCKETS - 1)
    return jnp.where(n < max_exact, n, large)


def _inproj_kernel(x_ref, g_ref, w_ref, o_ref):
    u = _rms(x_ref[0], g_ref[...]).astype(BF16)
    group = o_ref.shape[2] * LANES
    for c in range(o_ref.shape[0]):
        res = _dot(u, w_ref[:, c * group:(c + 1) * group])
        for hp in range(o_ref.shape[2]):
            o_ref[c, 0, hp] = res[:, hp * LANES:(hp + 1) * LANES].astype(BF16)


def _inproj(x, g, w):
    B, S, D = x.shape
    n_groups = w.shape[1] // (4 * LANES)
    return pl.pallas_call(
        _inproj_kernel,
        grid=(B, S // ROW_TILE),
        in_specs=[
            pl.BlockSpec((1, ROW_TILE, D), lambda b, t: (b, t, 0)),
            pl.BlockSpec((1, D), lambda b, t: (0, 0)),
            pl.BlockSpec(w.shape, lambda b, t: (0, 0)),
        ],
        out_specs=pl.BlockSpec((n_groups, 1, 4, ROW_TILE, LANES), lambda b, t: (0, b, 0, t, 0)),
        out_shape=jax.ShapeDtypeStruct((n_groups, B, 4, S, LANES), BF16),
        compiler_params=pltpu.CompilerParams(vmem_limit_bytes=VMEM_LIMIT),
        name="inproj",
    )(x, g, w)


def _score_stage(qt, kt, bias2, lane_lo):
    out = []
    for h in range(HEADS_PER_SLAB):
        in_head = lane_lo if h == 0 else jnp.logical_not(lane_lo)
        out.append(_nt_dot(jnp.where(in_head, qt, jnp.zeros_like(qt)), kt) + bias2[h])
    return out


def _softmax_stage(scores):
    out = []
    for s in scores:
        m = jnp.max(s, axis=-1, keepdims=True)
        p = jnp.exp(s - m)
        out.append((m, jnp.sum(p, axis=-1, keepdims=True), p.astype(BF16)))
    return out


def _value_stage(soft, vt, lane_lo):
    (m0, l0, p0), (m1, l1, p1) = soft
    return (jnp.where(lane_lo, m0, m1), jnp.where(lane_lo, l0, l1), jnp.where(lane_lo, _dot(p0, vt), _dot(p1, vt)))


def _run_tiles(tiles, lane_lo):
    scores = {}
    for step in range(len(tiles) + SCORE_LOOKAHEAD):
        done = step - SCORE_LOOKAHEAD
        if done >= 0:
            soft = _softmax_stage(scores.pop(done))
        if step < len(tiles):
            q, k, bias2 = tiles[step][0]()
            scores[step] = _score_stage(q, k, bias2, lane_lo)
        if done >= 0:
            tiles[done][2](_value_stage(soft, tiles[done][1](), lane_lo))


def _merge(old, new):
    m0, l0, a0 = old
    m1, l1, a1 = new
    m = jnp.maximum(m0, m1)
    e0 = jnp.exp(m0 - m)
    e1 = jnp.exp(m1 - m)
    return m, e0 * l0 + e1 * l1, e0 * a0 + e1 * a1


def _dilated_kernel(q_ref, k_ref, v_ref, bias_ref, o_ref, f_s, lo_s, lo_b, hi_b, nat_s, st0_s, st1_s, st2_s,
                    out_hi, out_lo):
    S = q_ref.shape[0]
    d_lo, d_hi = DILATION_PAIRS[1][1], DILATION_PAIRS[2][1]
    ratio = d_hi // d_lo
    assert DILATION_PAIRS[0][1] == 1 and d_hi == ratio * d_lo
    lane_lo = lax.broadcasted_iota(jnp.int32, (1, LANES), 1) < HEAD_DIM
    hi_rows = lambda r: (r % d_lo, pl.ds(r // d_lo, S // d_hi, stride=ratio))

    def make_tile(p, n, q_seq, k_seq, v_seq, finish):
        rows = slice(n * W_SUB, (n + 1) * W_SUB)
        if n == 0:
            keys = slice(0, W_SUB)
            bias2 = lambda: tuple(bias_ref[p, h, :, W_SUB:] for h in range(HEADS_PER_SLAB))
        else:
            keys = slice((n - 1) * W_SUB, (n + 1) * W_SUB)
            bias2 = lambda: tuple(bias_ref[p, h] for h in range(HEADS_PER_SLAB))
        return (lambda: (q_seq[rows], k_seq[keys], bias2()), lambda: v_seq[keys],
                functools.partial(finish, rows))

    for t, ref in enumerate((q_ref, k_ref, v_ref)):
        f_s[t] = ref[...].astype(F32)
        for r in range(d_lo):
            lo_s[t, r] = f_s[t, pl.ds(r, S // d_lo, stride=d_lo), :]
            lo_b[t, r] = lo_s[t, r].astype(BF16)
        for r in range(d_hi):
            r_lo, rows = hi_rows(r)
            hi_b[t, r] = lo_s[t, r_lo, rows, :].astype(BF16)

    def finish_d1(rows, res):
        for t in range(3):
            nat_s[t, rows] = res[t]

    _run_tiles([make_tile(0, n, q_ref, k_ref, v_ref, finish_d1) for n in range(S // W_SUB)], lane_lo)

    for t in range(3):
        for r in range(d_lo):
            st0_s[t, r] = nat_s[t, pl.ds(r, S // d_lo, stride=d_lo), :]

    def finish_lo(r, rows, res):
        merged = _merge(tuple(st0_s[t, r, rows] for t in range(3)), res)
        for t in range(3):
            st1_s[t, r, rows] = merged[t]

    _run_tiles([make_tile(1, n, lo_b.at[0, r], lo_b.at[1, r], lo_b.at[2, r], functools.partial(finish_lo, r))
                for r in range(d_lo) for n in range(S // d_lo // W_SUB)], lane_lo)

    for t in range(3):
        for r in range(d_hi):
            r_lo, rows = hi_rows(r)
            st2_s[t, r] = st1_s[t, r_lo, rows, :]

    def finish_hi(r, rows, res):
        _, l, a = _merge(tuple(st2_s[t, r, rows] for t in range(3)), res)
        out_hi[r, rows] = a / l

    _run_tiles([make_tile(2, n, hi_b.at[0, r], hi_b.at[1, r], hi_b.at[2, r], functools.partial(finish_hi, r))
                for r in range(d_hi) for n in range(S // d_hi // W_SUB)], lane_lo)

    for r in range(d_hi):
        r_lo, rows = hi_rows(r)
        out_lo[r_lo, rows, :] = out_hi[r]
    for r in range(d_lo):
        o_ref[pl.ds(r, S // d_lo, stride=d_lo), :] = out_lo[r].astype(o_ref.dtype)


def _dilated_attention(qkv, bias):
    _, B, HP, S, _ = qkv.shape
    d_lo, d_hi = DILATION_PAIRS[1][1], DILATION_PAIRS[2][1]
    nat = (3, S, LANES)
    by_lo = (d_lo, S // d_lo, LANES)
    by_hi = (d_hi, S // d_hi, LANES)
    qkv_spec = lambda g: pl.BlockSpec((None, None, None, S, LANES), lambda hp, b, g=g: (g, b, hp, 0, 0))
    return pl.pallas_call(
        _dilated_kernel,
        grid=(HP, B),
        in_specs=[qkv_spec(0), qkv_spec(1), qkv_spec(2),
                  pl.BlockSpec((None,) + bias.shape[1:], lambda hp, b: (hp, 0, 0, 0, 0))],
        out_specs=pl.BlockSpec((None, None, S, LANES), lambda hp, b: (b, hp, 0, 0)),
        out_shape=jax.ShapeDtypeStruct((B, HP, S, LANES), F32),
        scratch_shapes=[pltpu.VMEM(nat, F32), pltpu.VMEM((3,) + by_lo, F32), pltpu.VMEM((3,) + by_lo, BF16),
                        pltpu.VMEM((3,) + by_hi, BF16), pltpu.VMEM(nat, F32), pltpu.VMEM((3,) + by_lo, F32),
                        pltpu.VMEM((3,) + by_lo, F32), pltpu.VMEM((3,) + by_hi, F32),
                        pltpu.VMEM(by_hi, F32), pltpu.VMEM(by_lo, F32)],
        compiler_params=pltpu.CompilerParams(vmem_limit_bytes=VMEM_LIMIT),
        name="dilated_attn",
    )(qkv, qkv, qkv, bias)


def _moba_kernel(q_ref, k_ref, v_ref, bias_ref, o_ref, qa_s, ka_s):
    S = q_ref.shape[0]
    nb = S // MOBA_BLOCK
    n_cmp = nb * nb
    lane = lax.broadcasted_iota(jnp.int32, (1, LANES), 1)
    lane_lo = lane < HEAD_DIM

    kmean = jnp.sum(k_ref[...].astype(F32).reshape(nb, MOBA_BLOCK, LANES), axis=1) * (1.0 / MOBA_BLOCK)
    pad = jnp.zeros((LANES - n_cmp, LANES), F32)
    rhs_a = jnp.concatenate([kmean] * nb + [pad], axis=0).astype(BF16)
    rhs_b = jnp.concatenate([jnp.broadcast_to(kmean[i:i + 1], (nb, LANES)) for i in range(nb)] + [pad],
                            axis=0).astype(BF16)
    ci = lax.shift_right_logical(lane, int(math.log2(nb)))
    cj = lane & (nb - 1)
    tie = jnp.where(ci < cj, 1.0, 0.0)
    rr = lax.broadcasted_iota(jnp.int32, (LANES, LANES), 0)
    cc = lax.broadcasted_iota(jnp.int32, (LANES, LANES), 1)
    zero = jnp.zeros((), BF16)

    in_head = (lane_lo, jnp.logical_not(lane_lo))
    pen_base = tuple(HEAD_DIM * (1 - h) for h in range(HEADS_PER_SLAB))
    block_rows = lambda i: slice(i * MOBA_BLOCK, (i + 1) * MOBA_BLOCK)
    units = [(i, h) for i in range(nb) for h in range(HEADS_PER_SLAB)]
    gated = [(i, h) for i, h in units if i > MOBA_TOPK]

    gates = {}
    for h in range(HEADS_PER_SLAB):
        rhs_a_h = jnp.where(in_head[h], rhs_a, zero)
        rhs_b_h = jnp.where(in_head[h], rhs_b, zero)
        for i in range(nb):
            ka_s[h, block_rows(i)] = jnp.where(in_head[h], k_ref[block_rows(i)],
                                               jnp.where(lane - pen_base[h] == i, 1.0, 0.0).astype(BF16))
            if (i, h) in gated:
                gates[i, h] = (_nt_dot(q_ref[block_rows(i)], rhs_a_h), _nt_dot(q_ref[block_rows(i)], rhs_b_h))
            else:
                qa_s[h, block_rows(i)] = jnp.where(in_head[h], q_ref[block_rows(i)], zero)
    ranks = {}
    for i, h in gated:
        ga, gb = gates.pop((i, h))
        past = jnp.where((ci < i) & (lane < n_cmp), 1.0, 0.0)
        beats = jnp.where(gb > ga, 1.0, jnp.where(gb == ga, tie, 0.0)) * past
        gather = jnp.where(((rr & (nb - 1)) == (cc - pen_base[h])) & (rr < n_cmp), 1.0, 0.0).astype(BF16)
        ranks[i, h] = _dot(beats.astype(BF16), gather)
    for i, h in gated:
        jl = lane - pen_base[h]
        pen_lanes = jnp.where((jl >= 0) & (jl < i), NEG_INF, 0.0)
        pen = jnp.where(ranks.pop((i, h)) < MOBA_TOPK - 0.5, 0.0, pen_lanes)
        qa_s[h, block_rows(i)] = jnp.where(in_head[h], q_ref[block_rows(i)], pen.astype(BF16))

    scores, outs = {}, {}
    for step in range(len(units) + SCORE_LOOKAHEAD):
        done = step - SCORE_LOOKAHEAD
        if done >= 0:
            s = scores.pop(done)
            m = jnp.max(s, axis=-1, keepdims=True)
            p = jnp.exp(s - m)
            l = jnp.sum(p, axis=-1, keepdims=True)
            p = p.astype(BF16)
        if step < len(units):
            i, h = units[step]
            scores[step] = (_nt_dot(qa_s[h, block_rows(i)], ka_s[h, 0:(i + 1) * MOBA_BLOCK])
                            + bias_ref[h, :, (nb - 1 - i) * MOBA_BLOCK:])
        if done >= 0:
            i, h = units[done]
            outs[h] = _dot(p, v_ref[0:(i + 1) * MOBA_BLOCK]) / l
            if h == HEADS_PER_SLAB - 1:
                o_ref[block_rows(i)] = jnp.where(lane_lo, outs[0], outs[1]).astype(o_ref.dtype)


def _moba_attention(qkv, bias):
    _, B, HP, S, _ = qkv.shape
    qkv_spec = lambda g: pl.BlockSpec((None, None, None, S, LANES), lambda hp, b, g=g: (g, b, hp, 0, 0))
    return pl.pallas_call(
        _moba_kernel,
        grid=(HP, B),
        in_specs=[qkv_spec(3), qkv_spec(4), qkv_spec(5),
                  pl.BlockSpec((None,) + bias.shape[1:], lambda hp, b: (hp, 0, 0, 0))],
        out_specs=pl.BlockSpec((None, None, S, LANES), lambda hp, b: (b, hp, 0, 0)),
        out_shape=jax.ShapeDtypeStruct((B, HP, S, LANES), F32),
        scratch_shapes=[pltpu.VMEM((HEADS_PER_SLAB, S, LANES), BF16)] * 2,
        compiler_params=pltpu.CompilerParams(vmem_limit_bytes=VMEM_LIMIT),
        name="moba_attn",
    )(qkv, qkv, qkv, bias)


def _outproj_kernel(ya_ref, yb_ref, x_ref, ga_ref, gb_ref, wout_ref, gc_ref, wq_ref, h_ref, qm_ref):
    ya = jnp.concatenate([ya_ref[0, hp] for hp in range(ya_ref.shape[1])], axis=-1)
    yb = jnp.concatenate([yb_ref[0, hp] for hp in range(yb_ref.shape[1])], axis=-1)
    y = jnp.concatenate([_rms(ya, ga_ref[...]), _rms(yb, gb_ref[...])], axis=-1).astype(BF16)
    h1 = x_ref[0] + _dot(y, wout_ref[...])
    h_ref[0] = h1
    c = _rms(h1, gc_ref[...]).astype(BF16)
    qm_ref[0] = _dot(c, wq_ref[...]).astype(BF16)


def _outproj(ya, yb, x, g_a, g_b, w_out, g_cross, w_q):
    B, S, D = x.shape
    HP = ya.shape[1]
    const = lambda shape: pl.BlockSpec(shape, lambda b, t: (0,) * len(shape))
    y_spec = pl.BlockSpec((1, HP, ROW_TILE, LANES), lambda b, t: (b, 0, t, 0))
    row_spec = pl.BlockSpec((1, ROW_TILE, D), lambda b, t: (b, t, 0))
    return pl.pallas_call(
        _outproj_kernel,
        grid=(B, S // ROW_TILE),
        in_specs=[y_spec, y_spec, row_spec, const(g_a.shape), const(g_b.shape), const(w_out.shape),
                  const(g_cross.shape), const(w_q.shape)],
        out_specs=[row_spec, row_spec],
        out_shape=[jax.ShapeDtypeStruct((B, S, D), F32), jax.ShapeDtypeStruct((B, S, D), BF16)],
        compiler_params=pltpu.CompilerParams(vmem_limit_bytes=VMEM_LIMIT),
        name="outproj",
    )(ya, yb, x, g_a, g_b, w_out, g_cross, w_q)


def _memkv_kernel(mem_ref, g_ref, w_ref, k_ref, v_ref):
    m = _rms(mem_ref[0], g_ref[...]).astype(BF16)
    kv = _dot(m, w_ref[...])
    d = k_ref.shape[-1]
    k_ref[0] = kv[:, :d].astype(BF16)
    v_ref[0] = kv[:, d:].astype(BF16)


def _memkv(mem, g, w_kv):
    B, M, D = mem.shape
    spec = pl.BlockSpec((1, M, D), lambda b: (b, 0, 0))
    return pl.pallas_call(
        _memkv_kernel,
        grid=(B,),
        in_specs=[spec, pl.BlockSpec(g.shape, lambda b: (0, 0)), pl.BlockSpec(w_kv.shape, lambda b: (0, 0))],
        out_specs=[spec, spec],
        out_shape=[jax.ShapeDtypeStruct((B, M, D), BF16)] * 2,
        compiler_params=pltpu.CompilerParams(vmem_limit_bytes=VMEM_LIMIT),
        name="memkv",
    )(mem, g, w_kv)


def _cross_kernel(qm_ref, km_ref, vm_ref, h_ref, wo_ref, o_ref):
    d_head = qm_ref.shape[-1] // N_HEADS_MEM
    outs = []
    for hd in range(N_HEADS_MEM):
        cols = slice(hd * d_head, (hd + 1) * d_head)
        s = _nt_dot(qm_ref[0, :, cols], km_ref[0, :, cols])
        m = jnp.max(s, axis=-1, keepdims=True)
        p = jnp.exp(s - m)
        l = jnp.sum(p, axis=-1, keepdims=True)
        outs.append((_dot(p.astype(BF16), vm_ref[0, :, cols]) / l).astype(BF16))
    o = jnp.concatenate(outs, axis=-1)
    o_ref[0] = h_ref[0] + _dot(o, wo_ref[...])


def _cross_attention(qm, km, vm, h1, w_o):
    B, S, D = h1.shape
    M = km.shape[1]
    row_spec = pl.BlockSpec((1, ROW_TILE, D), lambda b, t: (b, t, 0))
    mem_spec = pl.BlockSpec((1, M, D), lambda b, t: (b, 0, 0))
    return pl.pallas_call(
        _cross_kernel,
        grid=(B, S // ROW_TILE),
        in_specs=[row_spec, mem_spec, mem_spec, row_spec, pl.BlockSpec(w_o.shape, lambda b, t: (0, 0))],
        out_specs=row_spec,
        out_shape=jax.ShapeDtypeStruct((B, S, D), F32),
        compiler_params=pltpu.CompilerParams(vmem_limit_bytes=VMEM_LIMIT),
        name="cross_attn",
    )(qm, km, vm, h1, w_o)


def _ffn_kernel(h_ref, halo_ref, g_ref, wg_ref, wu_ref, cw_ref, cb_ref, wd_ref, gf_ref, o_ref, *, final_norm):
    tm = h_ref.shape[1]
    h2 = h_ref[0]
    halo = jnp.where(pl.program_id(1) > 0, halo_ref[0], 0.0)
    f = _rms(jnp.concatenate([halo, h2], axis=0), g_ref[...]).astype(BF16)
    acc = jnp.zeros(h2.shape, F32)
    d_ff = wg_ref.shape[1]
    for c in range(d_ff // FFN_CHUNK):
        cols = slice(c * FFN_CHUNK, (c + 1) * FFN_CHUNK)
        a = _dot(f, wg_ref[:, cols])
        a_m1 = pltpu.roll(a, 1, 0)[HALO_ROWS:]
        a_m2 = pltpu.roll(a, 2, 0)[HALO_ROWS:]
        conv = cb_ref[:, cols] + a_m2 * cw_ref[0:1, cols]
        conv = conv + a_m1 * cw_ref[1:2, cols]
        conv = conv + a[HALO_ROWS:] * cw_ref[2:3, cols]
        up = _dot(f[HALO_ROWS:], wu_ref[:, cols])
        gated = (conv * jax.nn.sigmoid(conv) * up).astype(BF16)
        acc = acc + _dot(gated, wd_ref[cols, :])
    out = h2 + acc
    if final_norm:
        out = _rms(out, gf_ref[...])
    o_ref[0] = out


def _ffn(h2, g_ffn, w_gate, w_up, conv_w, conv_b, w_down, g_final, final_norm):
    B, S, D = h2.shape
    assert HALO_ROWS >= CONV_WIDTH - 1 and CONV_WIDTH == 3
    const = lambda a: pl.BlockSpec(a.shape, lambda b, t: (0,) * a.ndim, pipeline_mode=pl.Buffered(1))
    row_spec = pl.BlockSpec((1, ROW_TILE, D), lambda b, t: (b, t, 0))
    halo_spec = pl.BlockSpec((1, HALO_ROWS, D),
                             lambda b, t: (b, jnp.maximum(t * (ROW_TILE // HALO_ROWS) - 1, 0), 0))
    return pl.pallas_call(
        functools.partial(_ffn_kernel, final_norm=final_norm),
        grid=(B, S // ROW_TILE),
        in_specs=[row_spec, halo_spec, const(g_ffn), const(w_gate), const(w_up), const(conv_w),
                  const(conv_b), const(w_down), const(g_final)],
        out_specs=row_spec,
        out_shape=jax.ShapeDtypeStruct((B, S, D), F32),
        compiler_params=pltpu.CompilerParams(vmem_limit_bytes=VMEM_LIMIT),
        name="conv_ffn",
    )(h2, h2, g_ffn, w_gate, w_up, conv_w, conv_b, w_down, g_final)


def _bias_lookup(table, bucket):
    onehot = (bucket.reshape(1, -1) == jnp.arange(N_BUCKETS)[:, None]).astype(F32)
    out = jnp.einsum("hk,kn->hn", table.astype(F32), onehot, precision=lax.Precision.HIGHEST)
    return out.reshape((table.shape[0],) + bucket.shape)


def _dilated_bias_tiles(table):
    i = jnp.arange(W_SUB)[:, None]
    j = jnp.arange(2 * W_SUB)[None, :]
    diff = W_SUB + i - j
    valid = (diff >= 0) & (diff <= W_SUB)
    buckets = jnp.stack([_rel_bucket(diff * d) for _, d in DILATION_PAIRS], axis=0)
    t = jnp.where(valid, _bias_lookup(table, buckets), NEG_INF)
    t = t.reshape(N_HEADS_DIL // HEADS_PER_SLAB, HEADS_PER_SLAB, len(DILATION_PAIRS), W_SUB, 2 * W_SUB)
    return t.transpose(0, 2, 1, 3, 4)


def _moba_bias_slabs(table, S):
    a = jnp.arange(MOBA_BLOCK)[:, None]
    c = jnp.arange(S)[None, :]
    dist = a - c + (S - MOBA_BLOCK)
    t = jnp.where(dist >= 0, _bias_lookup(table, _rel_bucket(dist)), NEG_INF)
    return t.reshape(N_HEADS_MOBA // HEADS_PER_SLAB, HEADS_PER_SLAB, MOBA_BLOCK, S)


def kernel(x, mem, w_in, g_mix, g_out_dil, g_out_moba, w_out, rel_bias, g_cross, g_mem, w_q_mem, w_kv_mem,
           w_o_mem, g_ffn, w_gate, w_up, conv_w, conv_b, w_down, g_final):
    B, S, D = x.shape
    depth = w_in.shape[0]
    dil_width = N_HEADS_DIL * HEAD_DIM
    mix_width = dil_width + N_HEADS_MOBA * HEAD_DIM
    assert S % (W_SUB * DILATION_PAIRS[-1][1]) == 0 and S % MOBA_BLOCK == 0 and S % ROW_TILE == 0
    assert all(w // d == W_SUB for w, d in DILATION_PAIRS) and w_in.shape[2] == 3 * mix_width

    dil_bias = _dilated_bias_tiles(rel_bias[:N_HEADS_DIL])
    moba_bias = _moba_bias_slabs(rel_bias[N_HEADS_DIL:], S)
    col = jnp.arange(3 * mix_width)
    moba_width = mix_width - dil_width
    is_q = (col < dil_width) | ((col >= 3 * dil_width) & (col < 3 * dil_width + moba_width))
    in_scale = jnp.where(is_q, HEAD_DIM ** -0.5, 1.0).astype(F32)
    row2 = lambda v: v.reshape(1, -1)

    h = x
    for l in range(depth):
        qkv = _inproj(h, row2(g_mix[l]), (w_in[l] * in_scale).astype(BF16))
        ya = _dilated_attention(qkv, dil_bias)
        yb = _moba_attention(qkv, moba_bias)
        w_q = (w_q_mem[l] * ((D // N_HEADS_MEM) ** -0.5)).astype(BF16)
        h1, qm = _outproj(ya, yb, h, row2(g_out_dil[l]), row2(g_out_moba[l]), w_out[l].astype(BF16),
                          row2(g_cross[l]), w_q)
        km, vm = _memkv(mem, row2(g_mem[l]), w_kv_mem[l].astype(BF16))
        h2 = _cross_attention(qm, km, vm, h1, w_o_mem[l].astype(BF16))
        h = _ffn(h2, row2(g_ffn[l]), w_gate[l].astype(BF16), w_up[l].astype(BF16), conv_w[l], row2(conv_b[l]),
                 w_down[l].astype(BF16), row2(g_final), final_norm=(l == depth - 1))
    return h
```

```python
import functools
import math

import jax
import jax.numpy as jnp
from jax import lax
from jax.experimental import pallas as pl
from jax.experimental.pallas import tpu as pltpu

F32 = jnp.float32
BF16 = jnp.bfloat16

HEAD_DIM = 64
LANES = 128
HEADS_PER_SLAB = LANES // HEAD_DIM
N_HEADS_DIL = 8
N_HEADS_MOBA = 8
DILATION_PAIRS = ((128, 1), (512, 4), (2048, 16))
W_SUB = 128
MOBA_BLOCK = 256
MOBA_TOPK = 3
N_BUCKETS = 32
MAX_DISTANCE = 2048
N_HEADS_MEM = 4
CONV_WIDTH = 3
EPS = 1e-6
NEG_INF = -1e30

ROW_TILE = 1024
ROW_SPLIT = 2
FFN_CHUNK = 256
SCORE_LOOKAHEAD = 3
MOBA_LOOKAHEAD = 2
FFN_LOOKAHEAD = 1
HALO_ROWS = 16
VMEM_LIMIT = 56 * 1024 * 1024


def _rms(x, g):
    return x * lax.rsqrt(jnp.mean(x * x, axis=-1, keepdims=True) + EPS) * g


def _nt_dot(a, b):
    return lax.dot_general(a, b, (((1,), (1,)), ((), ())), preferred_element_type=F32)


def _dot(a, b):
    return jnp.dot(a, b, preferred_element_type=F32)


def _interleave(parts):
    parts = list(parts)
    while parts:
        for part in list(parts):
            try:
                next(part)
            except StopIteration:
                parts.remove(part)


def _rel_bucket(dist):
    max_exact = N_BUCKETS // 2
    n = jnp.maximum(dist, 0)
    nf = jnp.maximum(n, 1).astype(jnp.float32)
    large = max_exact + (jnp.log(nf / max_exact) / math.log(MAX_DISTANCE / max_exact)
                         * (N_BUCKETS - max_exact)).astype(jnp.int32)
    large = jnp.minimum(large, N_BUCKETS - 1)
    return jnp.where(n < max_exact, n, large)


def _inproj_kernel(x_ref, g_ref, w_ref, o_ref):
    sub = x_ref.shape[1] // ROW_SPLIT
    group = o_ref.shape[2] * LANES

    def part(k):
        rows = slice(k * sub, (k + 1) * sub)
        u = _rms(x_ref[0, rows], g_ref[...]).astype(BF16)
        for c in range(o_ref.shape[0]):
            res = _dot(u, w_ref[:, c * group:(c + 1) * group])
            yield
            for hp in range(o_ref.shape[2]):
                o_ref[c, 0, hp, rows] = res[:, hp * LANES:(hp + 1) * LANES].astype(BF16)

    _interleave(part(k) for k in range(ROW_SPLIT))


def _inproj(x, g, w):
    B, S, D = x.shape
    n_groups = w.shape[1] // (4 * LANES)
    return pl.pallas_call(
        _inproj_kernel,
        grid=(B, S // ROW_TILE),
        in_specs=[
            pl.BlockSpec((1, ROW_TILE, D), lambda b, t: (b, t, 0)),
            pl.BlockSpec((1, D), lambda b, t: (0, 0)),
            pl.BlockSpec(w.shape, lambda b, t: (0, 0), pipeline_mode=pl.Buffered(1)),
        ],
        out_specs=pl.BlockSpec((n_groups, 1, 4, ROW_TILE, LANES), lambda b, t: (0, b, 0, t, 0)),
        out_shape=jax.ShapeDtypeStruct((n_groups, B, 4, S, LANES), BF16),
        compiler_params=pltpu.CompilerParams(vmem_limit_bytes=VMEM_LIMIT),
        name="inproj",
    )(x, g, w)


def _score_stage(qt, kt, bias2, lane_lo):
    out = []
    for h in range(HEADS_PER_SLAB):
        in_head = lane_lo if h == 0 else jnp.logical_not(lane_lo)
        out.append(_nt_dot(jnp.where(in_head, qt, jnp.zeros_like(qt)), kt) + bias2[h])
    return out


def _softmax_stage(scores):
    out = []
    for s in scores:
        m = jnp.max(s, axis=-1, keepdims=True)
        p = jnp.exp(s - m)
        out.append((m, jnp.sum(p, axis=-1, keepdims=True), p.astype(BF16)))
    return out


def _value_stage(soft, vt, lane_lo):
    (m0, l0, p0), (m1, l1, p1) = soft
    return (jnp.where(lane_lo, m0, m1), jnp.where(lane_lo, l0, l1), jnp.where(lane_lo, _dot(p0, vt), _dot(p1, vt)))


def _run_tiles(tiles, lane_lo):
    scores = {}
    for step in range(len(tiles) + SCORE_LOOKAHEAD):
        done = step - SCORE_LOOKAHEAD
        if done >= 0:
            soft = _softmax_stage(scores.pop(done))
        if step < len(tiles):
            q, k, bias2 = tiles[step][0]()
            scores[step] = _score_stage(q, k, bias2, lane_lo)
        if done >= 0:
            tiles[done][2](_value_stage(soft, tiles[done][1](), lane_lo))


def _merge(old, new):
    m0, l0, a0 = old
    m1, l1, a1 = new
    m = jnp.maximum(m0, m1)
    e0 = jnp.exp(m0 - m)
    e1 = jnp.exp(m1 - m)
    return m, e0 * l0 + e1 * l1, e0 * a0 + e1 * a1


def _dilated_kernel(q_ref, k_ref, v_ref, bias_ref, o_ref, f_s, lo_s, lo_b, hi_b, nat_s, st0_s, st1_s, st2_s,
                    out_hi, out_lo):
    S = q_ref.shape[0]
    d_lo, d_hi = DILATION_PAIRS[1][1], DILATION_PAIRS[2][1]
    ratio = d_hi // d_lo
    assert DILATION_PAIRS[0][1] == 1 and d_hi == ratio * d_lo
    lane_lo = lax.broadcasted_iota(jnp.int32, (1, LANES), 1) < HEAD_DIM
    hi_rows = lambda r: (r % d_lo, pl.ds(r // d_lo, S // d_hi, stride=ratio))

    def make_tile(p, n, q_seq, k_seq, v_seq, finish):
        rows = slice(n * W_SUB, (n + 1) * W_SUB)
        if n == 0:
            keys = slice(0, W_SUB)
            bias2 = lambda: tuple(bias_ref[p, h, :, W_SUB:] for h in range(HEADS_PER_SLAB))
        else:
            keys = slice((n - 1) * W_SUB, (n + 1) * W_SUB)
            bias2 = lambda: tuple(bias_ref[p, h] for h in range(HEADS_PER_SLAB))
        return (lambda: (q_seq[rows], k_seq[keys], bias2()), lambda: v_seq[keys],
                functools.partial(finish, rows))

    for t, ref in enumerate((q_ref, k_ref, v_ref)):
        f_s[t] = ref[...].astype(F32)
        for r in range(d_lo):
            lo_s[t, r] = f_s[t, pl.ds(r, S // d_lo, stride=d_lo), :]
            lo_b[t, r] = lo_s[t, r].astype(BF16)
        for r in range(d_hi):
            r_lo, rows = hi_rows(r)
            hi_b[t, r] = lo_s[t, r_lo, rows, :].astype(BF16)

    def finish_d1(rows, res):
        for t in range(3):
            nat_s[t, rows] = res[t]

    _run_tiles([make_tile(0, n, q_ref, k_ref, v_ref, finish_d1) for n in range(S // W_SUB)], lane_lo)

    for t in range(3):
        for r in range(d_lo):
            st0_s[t, r] = nat_s[t, pl.ds(r, S // d_lo, stride=d_lo), :]

    def finish_lo(r, rows, res):
        merged = _merge(tuple(st0_s[t, r, rows] for t in range(3)), res)
        for t in range(3):
            st1_s[t, r, rows] = merged[t]

    _run_tiles([make_tile(1, n, lo_b.at[0, r], lo_b.at[1, r], lo_b.at[2, r], functools.partial(finish_lo, r))
                for r in range(d_lo) for n in range(S // d_lo // W_SUB)], lane_lo)

    for t in range(3):
        for r in range(d_hi):
            r_lo, rows = hi_rows(r)
            st2_s[t, r] = st1_s[t, r_lo, rows, :]

    def finish_hi(r, rows, res):
        _, l, a = _merge(tuple(st2_s[t, r, rows] for t in range(3)), res)
        out_hi[r, rows] = a / l

    _run_tiles([make_tile(2, n, hi_b.at[0, r], hi_b.at[1, r], hi_b.at[2, r], functools.partial(finish_hi, r))
                for r in range(d_hi) for n in range(S // d_hi // W_SUB)], lane_lo)

    for r in range(d_hi):
        r_lo, rows = hi_rows(r)
        out_lo[r_lo, rows, :] = out_hi[r]
    for r in range(d_lo):
        o_ref[pl.ds(r, S // d_lo, stride=d_lo), :] = out_lo[r].astype(o_ref.dtype)


def _dilated_attention(qkv, bias):
    _, B, HP, S, _ = qkv.shape
    d_lo, d_hi = DILATION_PAIRS[1][1], DILATION_PAIRS[2][1]
    nat = (3, S, LANES)
    by_lo = (d_lo, S // d_lo, LANES)
    by_hi = (d_hi, S // d_hi, LANES)
    qkv_spec = lambda g: pl.BlockSpec((None, None, None, S, LANES), lambda hp, b, g=g: (g, b, hp, 0, 0))
    return pl.pallas_call(
        _dilated_kernel,
        grid=(HP, B),
        in_specs=[qkv_spec(0), qkv_spec(1), qkv_spec(2),
                  pl.BlockSpec((None,) + bias.shape[1:], lambda hp, b: (hp, 0, 0, 0, 0))],
        out_specs=pl.BlockSpec((None, None, S, LANES), lambda hp, b: (b, hp, 0, 0)),
        out_shape=jax.ShapeDtypeStruct((B, HP, S, LANES), F32),
        scratch_shapes=[pltpu.VMEM(nat, F32), pltpu.VMEM((3,) + by_lo, F32), pltpu.VMEM((3,) + by_lo, BF16),
                        pltpu.VMEM((3,) + by_hi, BF16), pltpu.VMEM(nat, F32), pltpu.VMEM((3,) + by_lo, F32),
                        pltpu.VMEM((3,) + by_lo, F32), pltpu.VMEM((3,) + by_hi, F32),
                        pltpu.VMEM(by_hi, F32), pltpu.VMEM(by_lo, F32)],
        compiler_params=pltpu.CompilerParams(vmem_limit_bytes=VMEM_LIMIT),
        name="dilated_attn",
    )(qkv, qkv, qkv, bias)


def _moba_kernel(q_ref, k_ref, v_ref, bias_ref, o_ref, qa_s, ka_s):
    S = q_ref.shape[0]
    nb = S // MOBA_BLOCK
    n_cmp = nb * nb
    lane = lax.broadcasted_iota(jnp.int32, (1, LANES), 1)
    lane_lo = lane < HEAD_DIM

    kmean = jnp.sum(k_ref[...].astype(F32).reshape(nb, MOBA_BLOCK, LANES), axis=1) * (1.0 / MOBA_BLOCK)
    pad = jnp.zeros((LANES - n_cmp, LANES), F32)
    rhs_a = jnp.concatenate([kmean] * nb + [pad], axis=0).astype(BF16)
    rhs_b = jnp.concatenate([jnp.broadcast_to(kmean[i:i + 1], (nb, LANES)) for i in range(nb)] + [pad],
                            axis=0).astype(BF16)
    ci = lax.shift_right_logical(lane, int(math.log2(nb)))
    cj = lane & (nb - 1)
    tie = jnp.where(ci < cj, 1.0, 0.0)
    rr = lax.broadcasted_iota(jnp.int32, (LANES, LANES), 0)
    cc = lax.broadcasted_iota(jnp.int32, (LANES, LANES), 1)
    zero = jnp.zeros((), BF16)

    in_head = (lane_lo, jnp.logical_not(lane_lo))
    pen_base = tuple(HEAD_DIM * (1 - h) for h in range(HEADS_PER_SLAB))
    block_rows = lambda i: slice(i * MOBA_BLOCK, (i + 1) * MOBA_BLOCK)
    units = [(i, h) for i in range(nb) for h in range(HEADS_PER_SLAB)]
    gated = [(i, h) for i, h in units if i > MOBA_TOPK]

    gates = {}
    for h in range(HEADS_PER_SLAB):
        rhs_a_h = jnp.where(in_head[h], rhs_a, zero)
        rhs_b_h = jnp.where(in_head[h], rhs_b, zero)
        for i in range(nb):
            ka_s[h, block_rows(i)] = jnp.where(in_head[h], k_ref[block_rows(i)],
                                               jnp.where(lane - pen_base[h] == i, 1.0, 0.0).astype(BF16))
            if (i, h) in gated:
                gates[i, h] = (_nt_dot(q_ref[block_rows(i)], rhs_a_h), _nt_dot(q_ref[block_rows(i)], rhs_b_h))
            else:
                qa_s[h, block_rows(i)] = jnp.where(in_head[h], q_ref[block_rows(i)], zero)
    ranks = {}
    for i, h in gated:
        ga, gb = gates.pop((i, h))
        past = jnp.where((ci < i) & (lane < n_cmp), 1.0, 0.0)
        beats = jnp.where(gb > ga, 1.0, jnp.where(gb == ga, tie, 0.0)) * past
        gather = jnp.where(((rr & (nb - 1)) == (cc - pen_base[h])) & (rr < n_cmp), 1.0, 0.0).astype(BF16)
        ranks[i, h] = _dot(beats.astype(BF16), gather)
    for i, h in gated:
        jl = lane - pen_base[h]
        pen_lanes = jnp.where((jl >= 0) & (jl < i), NEG_INF, 0.0)
        pen = jnp.where(ranks.pop((i, h)) < MOBA_TOPK - 0.5, 0.0, pen_lanes)
        qa_s[h, block_rows(i)] = jnp.where(in_head[h], q_ref[block_rows(i)], pen.astype(BF16))

    scores, outs = {}, {}
    for step in range(len(units) + MOBA_LOOKAHEAD):
        done = step - MOBA_LOOKAHEAD
        if done >= 0:
            s = scores.pop(done)
            m = jnp.max(s, axis=-1, keepdims=True)
            p = jnp.exp(s - m)
            l = jnp.sum(p, axis=-1, keepdims=True)
            p = p.astype(BF16)
        if step < len(units):
            i, h = units[step]
            scores[step] = (_nt_dot(qa_s[h, block_rows(i)], ka_s[h, 0:(i + 1) * MOBA_BLOCK])
                            + bias_ref[h, :, (nb - 1 - i) * MOBA_BLOCK:])
        if done >= 0:
            i, h = units[done]
            outs[h] = _dot(p, v_ref[0:(i + 1) * MOBA_BLOCK]) / l
            if h == HEADS_PER_SLAB - 1:
                o_ref[block_rows(i)] = jnp.where(lane_lo, outs[0], outs[1]).astype(o_ref.dtype)


def _moba_attention(qkv, bias):
    _, B, HP, S, _ = qkv.shape
    qkv_spec = lambda g: pl.BlockSpec((None, None, None, S, LANES), lambda hp, b, g=g: (g, b, hp, 0, 0))
    return pl.pallas_call(
        _moba_kernel,
        grid=(HP, B),
        in_specs=[qkv_spec(3), qkv_spec(4), qkv_spec(5),
                  pl.BlockSpec((None,) + bias.shape[1:], lambda hp, b: (hp, 0, 0, 0))],
        out_specs=pl.BlockSpec((None, None, S, LANES), lambda hp, b: (b, hp, 0, 0)),
        out_shape=jax.ShapeDtypeStruct((B, HP, S, LANES), F32),
        scratch_shapes=[pltpu.VMEM((HEADS_PER_SLAB, S, LANES), BF16)] * 2,
        compiler_params=pltpu.CompilerParams(vmem_limit_bytes=VMEM_LIMIT),
        name="moba_attn",
    )(qkv, qkv, qkv, bias)


def _memkv_kernel(mem_ref, g_ref, w_ref, k_ref, v_ref):
    m = _rms(mem_ref[0], g_ref[...]).astype(BF16)
    kv = _dot(m, w_ref[...])
    d = k_ref.shape[-1]
    k_ref[0] = kv[:, :d].astype(BF16)
    v_ref[0] = kv[:, d:].astype(BF16)


def _memkv(mem, g, w_kv):
    B, M, D = mem.shape
    spec = pl.BlockSpec((1, M, D), lambda b: (b, 0, 0))
    return pl.pallas_call(
        _memkv_kernel,
        grid=(B,),
        in_specs=[spec, pl.BlockSpec(g.shape, lambda b: (0, 0)), pl.BlockSpec(w_kv.shape, lambda b: (0, 0))],
        out_specs=[spec, spec],
        out_shape=[jax.ShapeDtypeStruct((B, M, D), BF16)] * 2,
        compiler_params=pltpu.CompilerParams(vmem_limit_bytes=VMEM_LIMIT),
        name="memkv",
    )(mem, g, w_kv)


def _mix_kernel(ya_ref, yb_ref, x_ref, ga_ref, gb_ref, wout_ref, gc_ref, wq_ref, km_ref, vm_ref, wo_ref, o_ref):
    sub = x_ref.shape[1] // ROW_SPLIT
    d_head = x_ref.shape[2] // N_HEADS_MEM

    def part(k):
        rows = slice(k * sub, (k + 1) * sub)
        ya = jnp.concatenate([ya_ref[0, hp, rows] for hp in range(ya_ref.shape[1])], axis=-1)
        yb = jnp.concatenate([yb_ref[0, hp, rows] for hp in range(yb_ref.shape[1])], axis=-1)
        y = jnp.concatenate([_rms(ya, ga_ref[...]), _rms(yb, gb_ref[...])], axis=-1).astype(BF16)
        h1 = x_ref[0, rows] + _dot(y, wout_ref[...])
        yield
        qm = _dot(_rms(h1, gc_ref[...]).astype(BF16), wq_ref[...]).astype(BF16)
        yield
        heads = [slice(hd * d_head, (hd + 1) * d_head) for hd in range(N_HEADS_MEM)]
        scores = [_nt_dot(qm[:, cols], km_ref[0, :, cols]) for cols in heads]
        yield
        outs = []
        for s, cols in zip(scores, heads):
            m = jnp.max(s, axis=-1, keepdims=True)
            p = jnp.exp(s - m)
            l = jnp.sum(p, axis=-1, keepdims=True)
            outs.append((_dot(p.astype(BF16), vm_ref[0, :, cols]) / l).astype(BF16))
        yield
        o_ref[0, rows] = h1 + _dot(jnp.concatenate(outs, axis=-1), wo_ref[...])

    _interleave(part(k) for k in range(ROW_SPLIT))


def _mix(ya, yb, x, g_a, g_b, w_out, g_cross, w_q, km, vm, w_o):
    B, S, D = x.shape
    HP = ya.shape[1]
    M = km.shape[1]
    const = lambda a: pl.BlockSpec(a.shape, lambda b, t: (0,) * a.ndim, pipeline_mode=pl.Buffered(1))
    y_spec = pl.BlockSpec((1, HP, ROW_TILE, LANES), lambda b, t: (b, 0, t, 0))
    row_spec = pl.BlockSpec((1, ROW_TILE, D), lambda b, t: (b, t, 0))
    mem_spec = pl.BlockSpec((1, M, D), lambda b, t: (b, 0, 0))
    return pl.pallas_call(
        _mix_kernel,
        grid=(B, S // ROW_TILE),
        in_specs=[y_spec, y_spec, row_spec, const(g_a), const(g_b), const(w_out), const(g_cross), const(w_q),
                  mem_spec, mem_spec, const(w_o)],
        out_specs=row_spec,
        out_shape=jax.ShapeDtypeStruct((B, S, D), F32),
        compiler_params=pltpu.CompilerParams(vmem_limit_bytes=VMEM_LIMIT),
        name="mix",
    )(ya, yb, x, g_a, g_b, w_out, g_cross, w_q, km, vm, w_o)


def _ffn_kernel(h_ref, halo_ref, g_ref, wg_ref, wu_ref, cw_ref, cb_ref, wd_ref, gf_ref, o_ref, *, final_norm):
    sub = h_ref.shape[1] // ROW_SPLIT
    n_chunks = wg_ref.shape[1] // FFN_CHUNK
    chunk_cols = lambda c: slice(c * FFN_CHUNK, (c + 1) * FFN_CHUNK)

    def conv_act(c, a, up):
        cols = chunk_cols(c)
        a_m1 = pltpu.roll(a, 1, 0)[HALO_ROWS:]
        a_m2 = pltpu.roll(a, 2, 0)[HALO_ROWS:]
        conv = cb_ref[:, cols] + a_m2 * cw_ref[0:1, cols]
        conv = conv + a_m1 * cw_ref[1:2, cols]
        conv = conv + a[HALO_ROWS:] * cw_ref[2:3, cols]
        return (conv * jax.nn.sigmoid(conv) * up).astype(BF16)

    def part(k):
        lo = k * sub
        if k == 0:
            prev = jnp.where(pl.program_id(1) > 0, halo_ref[0], 0.0)
        else:
            prev = h_ref[0, lo - HALO_ROWS:lo]
        h2 = h_ref[0, lo:lo + sub]
        f = _rms(jnp.concatenate([prev, h2], axis=0), g_ref[...]).astype(BF16)
        out = h2
        pending = {}
        for step in range(n_chunks + FFN_LOOKAHEAD):
            done = step - FFN_LOOKAHEAD
            if done >= 0:
                gated = conv_act(done, *pending.pop(done))
            if step < n_chunks:
                pending[step] = (_dot(f, wg_ref[:, chunk_cols(step)]), _dot(f[HALO_ROWS:], wu_ref[:, chunk_cols(step)]))
            if done >= 0:
                out = out + _dot(gated, wd_ref[chunk_cols(done), :])
            yield
        if final_norm:
            out = _rms(out, gf_ref[...])
        o_ref[0, lo:lo + sub] = out

    _interleave(part(k) for k in range(ROW_SPLIT))


def _ffn(h2, g_ffn, w_gate, w_up, conv_w, conv_b, w_down, g_final, final_norm):
    B, S, D = h2.shape
    assert HALO_ROWS >= CONV_WIDTH - 1 and CONV_WIDTH == 3
    const = lambda a: pl.BlockSpec(a.shape, lambda b, t: (0,) * a.ndim, pipeline_mode=pl.Buffered(1))
    row_spec = pl.BlockSpec((1, ROW_TILE, D), lambda b, t: (b, t, 0))
    halo_spec = pl.BlockSpec((1, HALO_ROWS, D),
                             lambda b, t: (b, jnp.maximum(t * (ROW_TILE // HALO_ROWS) - 1, 0), 0))
    return pl.pallas_call(
        functools.partial(_ffn_kernel, final_norm=final_norm),
        grid=(B, S // ROW_TILE),
        in_specs=[row_spec, halo_spec, const(g_ffn), const(w_gate), const(w_up), const(conv_w),
                  const(conv_b), const(w_down), const(g_final)],
        out_specs=row_spec,
        out_shape=jax.ShapeDtypeStruct((B, S, D), F32),
        compiler_params=pltpu.CompilerParams(vmem_limit_bytes=VMEM_LIMIT),
        name="conv_ffn",
    )(h2, h2, g_ffn, w_gate, w_up, conv_w, conv_b, w_down, g_final)


def _bias_lookup(table, bucket):
    onehot = (bucket.reshape(1, -1) == jnp.arange(N_BUCKETS)[:, None]).astype(F32)
    out = jnp.einsum("hk,kn->hn", table.astype(F32), onehot, precision=lax.Precision.HIGHEST)
    return out.reshape((table.shape[0],) + bucket.shape)


def _dilated_bias_tiles(table):
    i = jnp.arange(W_SUB)[:, None]
    j = jnp.arange(2 * W_SUB)[None, :]
    diff = W_SUB + i - j
    valid = (diff >= 0) & (diff <= W_SUB)
    buckets = jnp.stack([_rel_bucket(diff * d) for _, d in DILATION_PAIRS], axis=0)
    t = jnp.where(valid, _bias_lookup(table, buckets), NEG_INF)
    t = t.reshape(N_HEADS_DIL // HEADS_PER_SLAB, HEADS_PER_SLAB, len(DILATION_PAIRS), W_SUB, 2 * W_SUB)
    return t.transpose(0, 2, 1, 3, 4)


def _moba_bias_slabs(table, S):
    a = jnp.arange(MOBA_BLOCK)[:, None]
    c = jnp.arange(S)[None, :]
    dist = a - c + (S - MOBA_BLOCK)
    t = jnp.where(dist >= 0, _bias_lookup(table, _rel_bucket(dist)), NEG_INF)
    return t.reshape(N_HEADS_MOBA // HEADS_PER_SLAB, HEADS_PER_SLAB, MOBA_BLOCK, S)


def kernel(x, mem, w_in, g_mix, g_out_dil, g_out_moba, w_out, rel_bias, g_cross, g_mem, w_q_mem, w_kv_mem,
           w_o_mem, g_ffn, w_gate, w_up, conv_w, conv_b, w_down, g_final):
    B, S, D = x.shape
    depth = w_in.shape[0]
    dil_width = N_HEADS_DIL * HEAD_DIM
    mix_width = dil_width + N_HEADS_MOBA * HEAD_DIM
    assert S % (W_SUB * DILATION_PAIRS[-1][1]) == 0 and S % MOBA_BLOCK == 0 and S % ROW_TILE == 0
    assert all(w // d == W_SUB for w, d in DILATION_PAIRS) and w_in.shape[2] == 3 * mix_width

    dil_bias = _dilated_bias_tiles(rel_bias[:N_HEADS_DIL])
    moba_bias = _moba_bias_slabs(rel_bias[N_HEADS_DIL:], S)
    col = jnp.arange(3 * mix_width)
    moba_width = mix_width - dil_width
    is_q = (col < dil_width) | ((col >= 3 * dil_width) & (col < 3 * dil_width + moba_width))
    in_scale = jnp.where(is_q, HEAD_DIM ** -0.5, 1.0).astype(F32)
    row2 = lambda v: v.reshape(1, -1)

    h = x
    for l in range(depth):
        qkv = _inproj(h, row2(g_mix[l]), (w_in[l] * in_scale).astype(BF16))
        ya = _dilated_attention(qkv, dil_bias)
        yb = _moba_attention(qkv, moba_bias)
        w_q = (w_q_mem[l] * ((D // N_HEADS_MEM) ** -0.5)).astype(BF16)
        km, vm = _memkv(mem, row2(g_mem[l]), w_kv_mem[l].astype(BF16))
        h2 = _mix(ya, yb, h, row2(g_out_dil[l]), row2(g_out_moba[l]), w_out[l].astype(BF16), row2(g_cross[l]), w_q,
                  km, vm, w_o_mem[l].astype(BF16))
        h = _ffn(h2, row2(g_ffn[l]), w_gate[l].astype(BF16), w_up[l].astype(BF16), conv_w[l], row2(conv_b[l]),
                 w_down[l].astype(BF16), row2(g_final), final_norm=(l == depth - 1))
    return h
```

```python
import functools
import math

import jax
import jax.numpy as jnp
from jax import lax
from jax.experimental import pallas as pl
from jax.experimental.pallas import tpu as pltpu

F32 = jnp.float32
BF16 = jnp.bfloat16

HEAD_DIM = 64
LANES = 128
HEADS_PER_SLAB = LANES // HEAD_DIM
N_HEADS_DIL = 8
N_HEADS_MOBA = 8
DILATION_PAIRS = ((128, 1), (512, 4), (2048, 16))
W_SUB = 128
MOBA_BLOCK = 256
MOBA_TOPK = 3
N_BUCKETS = 32
MAX_DISTANCE = 2048
N_HEADS_MEM = 4
CONV_WIDTH = 3
EPS = 1e-6
NEG_INF = -1e30

ROW_TILE = 1024
ROW_SPLIT = 2
FFN_ROW_TILE = 512
FFN_ROW_SPLIT = 1
FFN_CHUNK = 256
SCORE_LOOKAHEAD = 3
MOBA_LOOKAHEAD = 1
FFN_LOOKAHEAD = 1
HALO_ROWS = 16
VMEM_LIMIT = 56 * 1024 * 1024


def _rms(x, g):
    return x * lax.rsqrt(jnp.mean(x * x, axis=-1, keepdims=True) + EPS) * g


def _nt_dot(a, b):
    return lax.dot_general(a, b, (((1,), (1,)), ((), ())), preferred_element_type=F32)


def _dot(a, b):
    return jnp.dot(a, b, preferred_element_type=F32)


def _interleave(parts):
    parts = list(parts)
    while parts:
        for part in list(parts):
            try:
                next(part)
            except StopIteration:
                parts.remove(part)


def _rel_bucket(dist):
    max_exact = N_BUCKETS // 2
    n = jnp.maximum(dist, 0)
    nf = jnp.maximum(n, 1).astype(jnp.float32)
    large = max_exact + (jnp.log(nf / max_exact) / math.log(MAX_DISTANCE / max_exact)
                         * (N_BUCKETS - max_exact)).astype(jnp.int32)
    large = jnp.minimum(large, N_BUCKETS - 1)
    return jnp.where(n < max_exact, n, large)


def _inproj_kernel(x_ref, g_ref, w_ref, o_ref):
    sub = x_ref.shape[1] // ROW_SPLIT
    group = o_ref.shape[2] * LANES

    def part(k):
        rows = slice(k * sub, (k + 1) * sub)
        u = _rms(x_ref[0, rows], g_ref[...]).astype(BF16)
        for c in range(o_ref.shape[0]):
            res = _dot(u, w_ref[:, c * group:(c + 1) * group])
            yield
            for hp in range(o_ref.shape[2]):
                o_ref[c, 0, hp, rows] = res[:, hp * LANES:(hp + 1) * LANES].astype(BF16)

    _interleave(part(k) for k in range(ROW_SPLIT))


def _inproj(x, g, w):
    B, S, D = x.shape
    n_groups = w.shape[1] // (4 * LANES)
    return pl.pallas_call(
        _inproj_kernel,
        grid=(B, S // ROW_TILE),
        in_specs=[
            pl.BlockSpec((1, ROW_TILE, D), lambda b, t: (b, t, 0)),
            pl.BlockSpec((1, D), lambda b, t: (0, 0)),
            pl.BlockSpec(w.shape, lambda b, t: (0, 0), pipeline_mode=pl.Buffered(1)),
        ],
        out_specs=pl.BlockSpec((n_groups, 1, 4, ROW_TILE, LANES), lambda b, t: (0, b, 0, t, 0)),
        out_shape=jax.ShapeDtypeStruct((n_groups, B, 4, S, LANES), BF16),
        compiler_params=pltpu.CompilerParams(vmem_limit_bytes=VMEM_LIMIT),
        name="inproj",
    )(x, g, w)


def _score_stage(qt, kt, bias2, lane_lo):
    out = []
    for h in range(HEADS_PER_SLAB):
        in_head = lane_lo if h == 0 else jnp.logical_not(lane_lo)
        out.append(_nt_dot(jnp.where(in_head, qt, jnp.zeros_like(qt)), kt) + bias2[h])
    return out


def _softmax_stage(scores):
    out = []
    for s in scores:
        m = jnp.max(s, axis=-1, keepdims=True)
        p = jnp.exp(s - m)
        out.append((m, jnp.sum(p, axis=-1, keepdims=True), p.astype(BF16)))
    return out


def _value_stage(soft, vt, lane_lo):
    (m0, l0, p0), (m1, l1, p1) = soft
    return (jnp.where(lane_lo, m0, m1), jnp.where(lane_lo, l0, l1), jnp.where(lane_lo, _dot(p0, vt), _dot(p1, vt)))


def _run_tiles(tiles, lane_lo):
    scores = {}
    for step in range(len(tiles) + SCORE_LOOKAHEAD):
        done = step - SCORE_LOOKAHEAD
        if done >= 0:
            soft = _softmax_stage(scores.pop(done))
        if step < len(tiles):
            q, k, bias2 = tiles[step][0]()
            scores[step] = _score_stage(q, k, bias2, lane_lo)
        if done >= 0:
            tiles[done][2](_value_stage(soft, tiles[done][1](), lane_lo))


def _merge(old, new):
    m0, l0, a0 = old
    m1, l1, a1 = new
    m = jnp.maximum(m0, m1)
    e0 = jnp.exp(m0 - m)
    e1 = jnp.exp(m1 - m)
    return m, e0 * l0 + e1 * l1, e0 * a0 + e1 * a1


def _dilated_kernel(q_ref, k_ref, v_ref, bias_ref, o_ref, f_s, lo_s, lo_b, hi_b, nat_s, st0_s, st1_s, st2_s,
                    out_hi, out_lo):
    S = q_ref.shape[0]
    d_lo, d_hi = DILATION_PAIRS[1][1], DILATION_PAIRS[2][1]
    ratio = d_hi // d_lo
    assert DILATION_PAIRS[0][1] == 1 and d_hi == ratio * d_lo
    lane_lo = lax.broadcasted_iota(jnp.int32, (1, LANES), 1) < HEAD_DIM
    hi_rows = lambda r: (r % d_lo, pl.ds(r // d_lo, S // d_hi, stride=ratio))

    def make_tile(p, n, q_seq, k_seq, v_seq, finish):
        rows = slice(n * W_SUB, (n + 1) * W_SUB)
        if n == 0:
            keys = slice(0, W_SUB)
            bias2 = lambda: tuple(bias_ref[p, h, :, W_SUB:] for h in range(HEADS_PER_SLAB))
        else:
            keys = slice((n - 1) * W_SUB, (n + 1) * W_SUB)
            bias2 = lambda: tuple(bias_ref[p, h] for h in range(HEADS_PER_SLAB))
        return (lambda: (q_seq[rows], k_seq[keys], bias2()), lambda: v_seq[keys],
                functools.partial(finish, rows))

    for t, ref in enumerate((q_ref, k_ref, v_ref)):
        f_s[t] = ref[...].astype(F32)
        for r in range(d_lo):
            lo_s[t, r] = f_s[t, pl.ds(r, S // d_lo, stride=d_lo), :]
            lo_b[t, r] = lo_s[t, r].astype(BF16)
        for r in range(d_hi):
            r_lo, rows = hi_rows(r)
            hi_b[t, r] = lo_s[t, r_lo, rows, :].astype(BF16)

    def finish_d1(rows, res):
        for t in range(3):
            nat_s[t, rows] = res[t]

    _run_tiles([make_tile(0, n, q_ref, k_ref, v_ref, finish_d1) for n in range(S // W_SUB)], lane_lo)

    for t in range(3):
        for r in range(d_lo):
            st0_s[t, r] = nat_s[t, pl.ds(r, S // d_lo, stride=d_lo), :]

    def finish_lo(r, rows, res):
        merged = _merge(tuple(st0_s[t, r, rows] for t in range(3)), res)
        for t in range(3):
            st1_s[t, r, rows] = merged[t]

    _run_tiles([make_tile(1, n, lo_b.at[0, r], lo_b.at[1, r], lo_b.at[2, r], functools.partial(finish_lo, r))
                for r in range(d_lo) for n in range(S // d_lo // W_SUB)], lane_lo)

    for t in range(3):
        for r in range(d_hi):
            r_lo, rows = hi_rows(r)
            st2_s[t, r] = st1_s[t, r_lo, rows, :]

    def finish_hi(r, rows, res):
        _, l, a = _merge(tuple(st2_s[t, r, rows] for t in range(3)), res)
        out_hi[r, rows] = a / l

    _run_tiles([make_tile(2, n, hi_b.at[0, r], hi_b.at[1, r], hi_b.at[2, r], functools.partial(finish_hi, r))
                for r in range(d_hi) for n in range(S // d_hi // W_SUB)], lane_lo)

    for r in range(d_hi):
        r_lo, rows = hi_rows(r)
        out_lo[r_lo, rows, :] = out_hi[r]
    for r in range(d_lo):
        o_ref[pl.ds(r, S // d_lo, stride=d_lo), :] = out_lo[r].astype(o_ref.dtype)


def _dilated_attention(qkv, bias):
    _, B, HP, S, _ = qkv.shape
    d_lo, d_hi = DILATION_PAIRS[1][1], DILATION_PAIRS[2][1]
    nat = (3, S, LANES)
    by_lo = (d_lo, S // d_lo, LANES)
    by_hi = (d_hi, S // d_hi, LANES)
    qkv_spec = lambda g: pl.BlockSpec((None, None, None, S, LANES), lambda hp, b, g=g: (g, b, hp, 0, 0))
    return pl.pallas_call(
        _dilated_kernel,
        grid=(HP, B),
        in_specs=[qkv_spec(0), qkv_spec(1), qkv_spec(2),
                  pl.BlockSpec((None,) + bias.shape[1:], lambda hp, b: (hp, 0, 0, 0, 0))],
        out_specs=pl.BlockSpec((None, None, S, LANES), lambda hp, b: (b, hp, 0, 0)),
        out_shape=jax.ShapeDtypeStruct((B, HP, S, LANES), F32),
        scratch_shapes=[pltpu.VMEM(nat, F32), pltpu.VMEM((3,) + by_lo, F32), pltpu.VMEM((3,) + by_lo, BF16),
                        pltpu.VMEM((3,) + by_hi, BF16), pltpu.VMEM(nat, F32), pltpu.VMEM((3,) + by_lo, F32),
                        pltpu.VMEM((3,) + by_lo, F32), pltpu.VMEM((3,) + by_hi, F32),
                        pltpu.VMEM(by_hi, F32), pltpu.VMEM(by_lo, F32)],
        compiler_params=pltpu.CompilerParams(vmem_limit_bytes=VMEM_LIMIT),
        name="dilated_attn",
    )(qkv, qkv, qkv, bias)


def _moba_kernel(q_ref, k_ref, v_ref, bias_ref, o_ref, qa_s, ka_s):
    S = q_ref.shape[0]
    nb = S // MOBA_BLOCK
    n_cmp = nb * nb
    lane = lax.broadcasted_iota(jnp.int32, (1, LANES), 1)
    lane_lo = lane < HEAD_DIM

    kmean = jnp.sum(k_ref[...].astype(F32).reshape(nb, MOBA_BLOCK, LANES), axis=1) * (1.0 / MOBA_BLOCK)
    pad = jnp.zeros((LANES - n_cmp, LANES), F32)
    rhs_a = jnp.concatenate([kmean] * nb + [pad], axis=0).astype(BF16)
    rhs_b = jnp.concatenate([jnp.broadcast_to(kmean[i:i + 1], (nb, LANES)) for i in range(nb)] + [pad],
                            axis=0).astype(BF16)
    ci = lax.shift_right_logical(lane, int(math.log2(nb)))
    cj = lane & (nb - 1)
    tie = jnp.where(ci < cj, 1.0, 0.0)
    rr = lax.broadcasted_iota(jnp.int32, (LANES, LANES), 0)
    cc = lax.broadcasted_iota(jnp.int32, (LANES, LANES), 1)
    zero = jnp.zeros((), BF16)

    in_head = (lane_lo, jnp.logical_not(lane_lo))
    pen_base = tuple(HEAD_DIM * (1 - h) for h in range(HEADS_PER_SLAB))
    block_rows = lambda i: slice(i * MOBA_BLOCK, (i + 1) * MOBA_BLOCK)
    units = [(i, h) for i in range(nb) for h in range(HEADS_PER_SLAB)]
    gated = [(i, h) for i, h in units if i > MOBA_TOPK]

    gates = {}
    for h in range(HEADS_PER_SLAB):
        rhs_a_h = jnp.where(in_head[h], rhs_a, zero)
        rhs_b_h = jnp.where(in_head[h], rhs_b, zero)
        for i in range(nb):
            ka_s[h, block_rows(i)] = jnp.where(in_head[h], k_ref[block_rows(i)],
                                               jnp.where(lane - pen_base[h] == i, 1.0, 0.0).astype(BF16))
            if (i, h) in gated:
                gates[i, h] = (_nt_dot(q_ref[block_rows(i)], rhs_a_h), _nt_dot(q_ref[block_rows(i)], rhs_b_h))
            else:
                qa_s[h, block_rows(i)] = jnp.where(in_head[h], q_ref[block_rows(i)], zero)

    def finish_gating():
        ranks = {}
        for i, h in gated:
            ga, gb = gates.pop((i, h))
            past = jnp.where((ci < i) & (lane < n_cmp), 1.0, 0.0)
            beats = jnp.where(gb > ga, 1.0, jnp.where(gb == ga, tie, 0.0)) * past
            gather = jnp.where(((rr & (nb - 1)) == (cc - pen_base[h])) & (rr < n_cmp), 1.0, 0.0).astype(BF16)
            ranks[i, h] = _dot(beats.astype(BF16), gather)
        for i, h in gated:
            jl = lane - pen_base[h]
            pen_lanes = jnp.where((jl >= 0) & (jl < i), NEG_INF, 0.0)
            pen = jnp.where(ranks.pop((i, h)) < MOBA_TOPK - 0.5, 0.0, pen_lanes)
            qa_s[h, block_rows(i)] = jnp.where(in_head[h], q_ref[block_rows(i)], pen.astype(BF16))

    scores, outs = {}, {}
    for step in range(len(units) + MOBA_LOOKAHEAD):
        if step < len(units) and gates and units[step] in gated:
            finish_gating()
        done = step - MOBA_LOOKAHEAD
        if done >= 0:
            s = scores.pop(done)
            m = jnp.max(s, axis=-1, keepdims=True)
            p = jnp.exp(s - m)
            l = jnp.sum(p, axis=-1, keepdims=True)
            p = p.astype(BF16)
        if step < len(units):
            i, h = units[step]
            scores[step] = (_nt_dot(qa_s[h, block_rows(i)], ka_s[h, 0:(i + 1) * MOBA_BLOCK])
                            + bias_ref[h, :, (nb - 1 - i) * MOBA_BLOCK:])
        if done >= 0:
            i, h = units[done]
            outs[h] = _dot(p, v_ref[0:(i + 1) * MOBA_BLOCK]) / l
            if h == HEADS_PER_SLAB - 1:
                o_ref[block_rows(i)] = jnp.where(lane_lo, outs[0], outs[1]).astype(o_ref.dtype)


def _moba_attention(qkv, bias):
    _, B, HP, S, _ = qkv.shape
    qkv_spec = lambda g: pl.BlockSpec((None, None, None, S, LANES), lambda hp, b, g=g: (g, b, hp, 0, 0))
    return pl.pallas_call(
        _moba_kernel,
        grid=(HP, B),
        in_specs=[qkv_spec(3), qkv_spec(4), qkv_spec(5),
                  pl.BlockSpec((None,) + bias.shape[1:], lambda hp, b: (hp, 0, 0, 0))],
        out_specs=pl.BlockSpec((None, None, S, LANES), lambda hp, b: (b, hp, 0, 0)),
        out_shape=jax.ShapeDtypeStruct((B, HP, S, LANES), F32),
        scratch_shapes=[pltpu.VMEM((HEADS_PER_SLAB, S, LANES), BF16)] * 2,
        compiler_params=pltpu.CompilerParams(vmem_limit_bytes=VMEM_LIMIT),
        name="moba_attn",
    )(qkv, qkv, qkv, bias)


def _memkv_kernel(mem_ref, g_ref, w_ref, k_ref, v_ref):
    m = _rms(mem_ref[0], g_ref[...]).astype(BF16)
    kv = _dot(m, w_ref[...])
    d = k_ref.shape[-1]
    k_ref[0] = kv[:, :d].astype(BF16)
    v_ref[0] = kv[:, d:].astype(BF16)


def _memkv(mem, g, w_kv):
    B, M, D = mem.shape
    spec = pl.BlockSpec((1, M, D), lambda b: (b, 0, 0))
    return pl.pallas_call(
        _memkv_kernel,
        grid=(B,),
        in_specs=[spec, pl.BlockSpec(g.shape, lambda b: (0, 0)), pl.BlockSpec(w_kv.shape, lambda b: (0, 0))],
        out_specs=[spec, spec],
        out_shape=[jax.ShapeDtypeStruct((B, M, D), BF16)] * 2,
        compiler_params=pltpu.CompilerParams(vmem_limit_bytes=VMEM_LIMIT),
        name="memkv",
    )(mem, g, w_kv)


def _mix_kernel(ya_ref, yb_ref, x_ref, ga_ref, gb_ref, wout_ref, gc_ref, wq_ref, km_ref, vm_ref, wo_ref, o_ref):
    sub = x_ref.shape[1] // ROW_SPLIT
    d_head = x_ref.shape[2] // N_HEADS_MEM

    def part(k):
        rows = slice(k * sub, (k + 1) * sub)
        ya = jnp.concatenate([ya_ref[0, hp, rows] for hp in range(ya_ref.shape[1])], axis=-1)
        yb = jnp.concatenate([yb_ref[0, hp, rows] for hp in range(yb_ref.shape[1])], axis=-1)
        y = jnp.concatenate([_rms(ya, ga_ref[...]), _rms(yb, gb_ref[...])], axis=-1).astype(BF16)
        h1 = x_ref[0, rows] + _dot(y, wout_ref[...])
        yield
        qm = _dot(_rms(h1, gc_ref[...]).astype(BF16), wq_ref[...]).astype(BF16)
        yield
        heads = [slice(hd * d_head, (hd + 1) * d_head) for hd in range(N_HEADS_MEM)]
        scores = [_nt_dot(qm[:, cols], km_ref[0, :, cols]) for cols in heads]
        yield
        outs = []
        for s, cols in zip(scores, heads):
            m = jnp.max(s, axis=-1, keepdims=True)
            p = jnp.exp(s - m)
            l = jnp.sum(p, axis=-1, keepdims=True)
            outs.append((_dot(p.astype(BF16), vm_ref[0, :, cols]) / l).astype(BF16))
        yield
        o_ref[0, rows] = h1 + _dot(jnp.concatenate(outs, axis=-1), wo_ref[...])

    _interleave(part(k) for k in range(ROW_SPLIT))


def _mix(ya, yb, x, g_a, g_b, w_out, g_cross, w_q, km, vm, w_o):
    B, S, D = x.shape
    HP = ya.shape[1]
    M = km.shape[1]
    const = lambda a: pl.BlockSpec(a.shape, lambda b, t: (0,) * a.ndim, pipeline_mode=pl.Buffered(1))
    y_spec = pl.BlockSpec((1, HP, ROW_TILE, LANES), lambda b, t: (b, 0, t, 0))
    row_spec = pl.BlockSpec((1, ROW_TILE, D), lambda b, t: (b, t, 0))
    mem_spec = pl.BlockSpec((1, M, D), lambda b, t: (b, 0, 0))
    return pl.pallas_call(
        _mix_kernel,
        grid=(B, S // ROW_TILE),
        in_specs=[y_spec, y_spec, row_spec, const(g_a), const(g_b), const(w_out), const(g_cross), const(w_q),
                  mem_spec, mem_spec, const(w_o)],
        out_specs=row_spec,
        out_shape=jax.ShapeDtypeStruct((B, S, D), F32),
        compiler_params=pltpu.CompilerParams(vmem_limit_bytes=VMEM_LIMIT),
        name="mix",
    )(ya, yb, x, g_a, g_b, w_out, g_cross, w_q, km, vm, w_o)


def _ffn_kernel(h_ref, halo_ref, g_ref, wg_ref, wu_ref, cw_ref, cb_ref, wd_ref, gf_ref, o_ref, *, final_norm):
    sub = h_ref.shape[1] // FFN_ROW_SPLIT
    n_chunks = wg_ref.shape[1] // FFN_CHUNK
    chunk_cols = lambda c: slice(c * FFN_CHUNK, (c + 1) * FFN_CHUNK)

    def conv_act(c, a, up):
        cols = chunk_cols(c)
        a_m1 = pltpu.roll(a, 1, 0)[HALO_ROWS:]
        a_m2 = pltpu.roll(a, 2, 0)[HALO_ROWS:]
        conv = cb_ref[:, cols] + a_m2 * cw_ref[0:1, cols]
        conv = conv + a_m1 * cw_ref[1:2, cols]
        conv = conv + a[HALO_ROWS:] * cw_ref[2:3, cols]
        return (conv * jax.nn.sigmoid(conv) * up).astype(BF16)

    def part(k):
        lo = k * sub
        if k == 0:
            prev = jnp.where(pl.program_id(1) > 0, halo_ref[0], 0.0)
        else:
            prev = h_ref[0, lo - HALO_ROWS:lo]
        h2 = h_ref[0, lo:lo + sub]
        f = _rms(jnp.concatenate([prev, h2], axis=0), g_ref[...]).astype(BF16)
        out = h2
        pending = {}
        for step in range(n_chunks + FFN_LOOKAHEAD):
            done = step - FFN_LOOKAHEAD
            if done >= 0:
                gated = conv_act(done, *pending.pop(done))
            if step < n_chunks:
                pending[step] = (_dot(f, wg_ref[:, chunk_cols(step)]), _dot(f[HALO_ROWS:], wu_ref[:, chunk_cols(step)]))
            if done >= 0:
                out = out + _dot(gated, wd_ref[chunk_cols(done), :])
            yield
        if final_norm:
            out = _rms(out, gf_ref[...])
        o_ref[0, lo:lo + sub] = out

    _interleave(part(k) for k in range(FFN_ROW_SPLIT))


def _ffn(h2, g_ffn, w_gate, w_up, conv_w, conv_b, w_down, g_final, final_norm):
    B, S, D = h2.shape
    assert HALO_ROWS >= CONV_WIDTH - 1 and CONV_WIDTH == 3
    const = lambda a: pl.BlockSpec(a.shape, lambda b, t: (0,) * a.ndim, pipeline_mode=pl.Buffered(1))
    row_spec = pl.BlockSpec((1, FFN_ROW_TILE, D), lambda b, t: (b, t, 0))
    halo_spec = pl.BlockSpec((1, HALO_ROWS, D),
                             lambda b, t: (b, jnp.maximum(t * (FFN_ROW_TILE // HALO_ROWS) - 1, 0), 0))
    return pl.pallas_call(
        functools.partial(_ffn_kernel, final_norm=final_norm),
        grid=(B, S // FFN_ROW_TILE),
        in_specs=[row_spec, halo_spec, const(g_ffn), const(w_gate), const(w_up), const(conv_w),
                  const(conv_b), const(w_down), const(g_final)],
        out_specs=row_spec,
        out_shape=jax.ShapeDtypeStruct((B, S, D), F32),
        compiler_params=pltpu.CompilerParams(vmem_limit_bytes=VMEM_LIMIT),
        name="conv_ffn",
    )(h2, h2, g_ffn, w_gate, w_up, conv_w, conv_b, w_down, g_final)


def _bias_lookup(table, bucket):
    onehot = (bucket.reshape(1, -1) == jnp.arange(N_BUCKETS)[:, None]).astype(F32)
    out = jnp.einsum("hk,kn->hn", table.astype(F32), onehot, precision=lax.Precision.HIGHEST)
    return out.reshape((table.shape[0],) + bucket.shape)


def _dilated_bias_tiles(table):
    i = jnp.arange(W_SUB)[:, None]
    j = jnp.arange(2 * W_SUB)[None, :]
    diff = W_SUB + i - j
    valid = (diff >= 0) & (diff <= W_SUB)
    buckets = jnp.stack([_rel_bucket(diff * d) for _, d in DILATION_PAIRS], axis=0)
    t = jnp.where(valid, _bias_lookup(table, buckets), NEG_INF)
    t = t.reshape(N_HEADS_DIL // HEADS_PER_SLAB, HEADS_PER_SLAB, len(DILATION_PAIRS), W_SUB, 2 * W_SUB)
    return t.transpose(0, 2, 1, 3, 4)


def _moba_bias_slabs(table, S):
    a = jnp.arange(MOBA_BLOCK)[:, None]
    c = jnp.arange(S)[None, :]
    dist = a - c + (S - MOBA_BLOCK)
    t = jnp.where(dist >= 0, _bias_lookup(table, _rel_bucket(dist)), NEG_INF)
    return t.reshape(N_HEADS_MOBA // HEADS_PER_SLAB, HEADS_PER_SLAB, MOBA_BLOCK, S)


def kernel(x, mem, w_in, g_mix, g_out_dil, g_out_moba, w_out, rel_bias, g_cross, g_mem, w_q_mem, w_kv_mem,
           w_o_mem, g_ffn, w_gate, w_up, conv_w, conv_b, w_down, g_final):
    B, S, D = x.shape
    depth = w_in.shape[0]
    dil_width = N_HEADS_DIL * HEAD_DIM
    mix_width = dil_width + N_HEADS_MOBA * HEAD_DIM
    assert S % (W_SUB * DILATION_PAIRS[-1][1]) == 0 and S % MOBA_BLOCK == 0 and S % ROW_TILE == 0
    assert all(w // d == W_SUB for w, d in DILATION_PAIRS) and w_in.shape[2] == 3 * mix_width

    dil_bias = _dilated_bias_tiles(rel_bias[:N_HEADS_DIL])
    moba_bias = _moba_bias_slabs(rel_bias[N_HEADS_DIL:], S)
    col = jnp.arange(3 * mix_width)
    moba_width = mix_width - dil_width
    is_q = (col < dil_width) | ((col >= 3 * dil_width) & (col < 3 * dil_width + moba_width))
    in_scale = jnp.where(is_q, HEAD_DIM ** -0.5, 1.0).astype(F32)
    row2 = lambda v: v.reshape(1, -1)

    h = x
    for l in range(depth):
        qkv = _inproj(h, row2(g_mix[l]), (w_in[l] * in_scale).astype(BF16))
        ya = _dilated_attention(qkv, dil_bias)
        yb = _moba_attention(qkv, moba_bias)
        w_q = (w_q_mem[l] * ((D // N_HEADS_MEM) ** -0.5)).astype(BF16)
        km, vm = _memkv(mem, row2(g_mem[l]), w_kv_mem[l].astype(BF16))
        h2 = _mix(ya, yb, h, row2(g_out_dil[l]), row2(g_out_moba[l]), w_out[l].astype(BF16), row2(g_cross[l]), w_q,
                  km, vm, w_o_mem[l].astype(BF16))
        h = _ffn(h2, row2(g_ffn[l]), w_gate[l].astype(BF16), w_up[l].astype(BF16), conv_w[l], row2(conv_b[l]),
                 w_down[l].astype(BF16), row2(g_final), final_norm=(l == depth - 1))
    return h
```

```python
import functools
import math

import jax
import jax.numpy as jnp
from jax import lax
from jax.experimental import pallas as pl
from jax.experimental.pallas import tpu as pltpu

F32 = jnp.float32
BF16 = jnp.bfloat16

HEAD_DIM = 64
LANES = 128
HEADS_PER_SLAB = LANES // HEAD_DIM
N_HEADS_DIL = 8
N_HEADS_MOBA = 8
DILATION_PAIRS = ((128, 1), (512, 4), (2048, 16))
W_SUB = 128
MOBA_BLOCK = 256
MOBA_TOPK = 3
N_BUCKETS = 32
MAX_DISTANCE = 2048
N_HEADS_MEM = 4
CONV_WIDTH = 3
EPS = 1e-6
NEG_INF = -1e30

ROW_TILE = 1024
ROW_SPLIT = 2
FFN_ROW_TILE = 512
FFN_ROW_SPLIT = 1
FFN_CHUNK = 256
SCORE_LOOKAHEAD = 3
MOBA_LOOKAHEAD = 1
HALO_ROWS = 16
VMEM_LIMIT = 56 * 1024 * 1024


def _rms(x, g):
    return x * lax.rsqrt(jnp.mean(x * x, axis=-1, keepdims=True) + EPS) * g


def _nt_dot(a, b):
    return lax.dot_general(a, b, (((1,), (1,)), ((), ())), preferred_element_type=F32)


def _dot(a, b):
    return jnp.dot(a, b, preferred_element_type=F32)


def _interleave(parts):
    parts = list(parts)
    while parts:
        for part in list(parts):
            try:
                next(part)
            except StopIteration:
                parts.remove(part)


def _rel_bucket(dist):
    max_exact = N_BUCKETS // 2
    n = jnp.maximum(dist, 0)
    nf = jnp.maximum(n, 1).astype(jnp.float32)
    large = max_exact + (jnp.log(nf / max_exact) / math.log(MAX_DISTANCE / max_exact)
                         * (N_BUCKETS - max_exact)).astype(jnp.int32)
    large = jnp.minimum(large, N_BUCKETS - 1)
    return jnp.where(n < max_exact, n, large)


def _inproj_kernel(x_ref, g_ref, w_ref, o_ref):
    sub = x_ref.shape[1] // ROW_SPLIT
    group = o_ref.shape[2] * LANES

    def part(k):
        rows = slice(k * sub, (k + 1) * sub)
        u = _rms(x_ref[0, rows], g_ref[...]).astype(BF16)
        for c in range(o_ref.shape[0]):
            res = _dot(u, w_ref[:, c * group:(c + 1) * group])
            yield
            for hp in range(o_ref.shape[2]):
                o_ref[c, 0, hp, rows] = res[:, hp * LANES:(hp + 1) * LANES].astype(BF16)

    _interleave(part(k) for k in range(ROW_SPLIT))


def _inproj(x, g, w):
    B, S, D = x.shape
    n_groups = w.shape[1] // (4 * LANES)
    return pl.pallas_call(
        _inproj_kernel,
        grid=(B, S // ROW_TILE),
        in_specs=[
            pl.BlockSpec((1, ROW_TILE, D), lambda b, t: (b, t, 0)),
            pl.BlockSpec((1, D), lambda b, t: (0, 0)),
            pl.BlockSpec(w.shape, lambda b, t: (0, 0), pipeline_mode=pl.Buffered(1)),
        ],
        out_specs=pl.BlockSpec((n_groups, 1, 4, ROW_TILE, LANES), lambda b, t: (0, b, 0, t, 0)),
        out_shape=jax.ShapeDtypeStruct((n_groups, B, 4, S, LANES), BF16),
        compiler_params=pltpu.CompilerParams(vmem_limit_bytes=VMEM_LIMIT),
        name="inproj",
    )(x, g, w)


def _score_stage(qt, kt, bias2, lane_lo):
    out = []
    for h in range(HEADS_PER_SLAB):
        in_head = lane_lo if h == 0 else jnp.logical_not(lane_lo)
        out.append(_nt_dot(jnp.where(in_head, qt, jnp.zeros_like(qt)), kt) + bias2[h])
    return out


def _softmax_stage(scores):
    out = []
    for s in scores:
        m = jnp.max(s, axis=-1, keepdims=True)
        p = jnp.exp(s - m)
        out.append((m, jnp.sum(p, axis=-1, keepdims=True), p.astype(BF16)))
    return out


def _value_stage(soft, vt, lane_lo):
    (m0, l0, p0), (m1, l1, p1) = soft
    return (jnp.where(lane_lo, m0, m1), jnp.where(lane_lo, l0, l1), jnp.where(lane_lo, _dot(p0, vt), _dot(p1, vt)))


def _run_tiles(tiles, lane_lo):
    scores = {}
    for step in range(len(tiles) + SCORE_LOOKAHEAD):
        done = step - SCORE_LOOKAHEAD
        if done >= 0:
            soft = _softmax_stage(scores.pop(done))
        if step < len(tiles):
            q, k, bias2 = tiles[step][0]()
            scores[step] = _score_stage(q, k, bias2, lane_lo)
        if done >= 0:
            tiles[done][2](_value_stage(soft, tiles[done][1](), lane_lo))


def _merge(old, new):
    m0, l0, a0 = old
    m1, l1, a1 = new
    m = jnp.maximum(m0, m1)
    e0 = jnp.exp(m0 - m)
    e1 = jnp.exp(m1 - m)
    return m, e0 * l0 + e1 * l1, e0 * a0 + e1 * a1


def _dilated_kernel(q_ref, k_ref, v_ref, bias_ref, o_ref, f_s, lo_s, lo_b, hi_b, nat_s, st0_s, st1_s, st2_s,
                    out_hi, out_lo):
    S = q_ref.shape[0]
    d_lo, d_hi = DILATION_PAIRS[1][1], DILATION_PAIRS[2][1]
    ratio = d_hi // d_lo
    assert DILATION_PAIRS[0][1] == 1 and d_hi == ratio * d_lo
    lane_lo = lax.broadcasted_iota(jnp.int32, (1, LANES), 1) < HEAD_DIM
    hi_rows = lambda r: (r % d_lo, pl.ds(r // d_lo, S // d_hi, stride=ratio))

    def make_tile(p, n, q_seq, k_seq, v_seq, finish):
        rows = slice(n * W_SUB, (n + 1) * W_SUB)
        if n == 0:
            keys = slice(0, W_SUB)
            bias2 = lambda: tuple(bias_ref[p, h, :, W_SUB:] for h in range(HEADS_PER_SLAB))
        else:
            keys = slice((n - 1) * W_SUB, (n + 1) * W_SUB)
            bias2 = lambda: tuple(bias_ref[p, h] for h in range(HEADS_PER_SLAB))
        return (lambda: (q_seq[rows], k_seq[keys], bias2()), lambda: v_seq[keys],
                functools.partial(finish, rows))

    for t, ref in enumerate((q_ref, k_ref, v_ref)):
        f_s[t] = ref[...].astype(F32)
        for r in range(d_lo):
            lo_s[t, r] = f_s[t, pl.ds(r, S // d_lo, stride=d_lo), :]
            lo_b[t, r] = lo_s[t, r].astype(BF16)
        for r in range(d_hi):
            r_lo, rows = hi_rows(r)
            hi_b[t, r] = lo_s[t, r_lo, rows, :].astype(BF16)

    def finish_d1(rows, res):
        for t in range(3):
            nat_s[t, rows] = res[t]

    _run_tiles([make_tile(0, n, q_ref, k_ref, v_ref, finish_d1) for n in range(S // W_SUB)], lane_lo)

    for t in range(3):
        for r in range(d_lo):
            st0_s[t, r] = nat_s[t, pl.ds(r, S // d_lo, stride=d_lo), :]

    def finish_lo(r, rows, res):
        merged = _merge(tuple(st0_s[t, r, rows] for t in range(3)), res)
        for t in range(3):
            st1_s[t, r, rows] = merged[t]

    _run_tiles([make_tile(1, n, lo_b.at[0, r], lo_b.at[1, r], lo_b.at[2, r], functools.partial(finish_lo, r))
                for r in range(d_lo) for n in range(S // d_lo // W_SUB)], lane_lo)

    for t in range(3):
        for r in range(d_hi):
            r_lo, rows = hi_rows(r)
            st2_s[t, r] = st1_s[t, r_lo, rows, :]

    def finish_hi(r, rows, res):
        _, l, a = _merge(tuple(st2_s[t, r, rows] for t in range(3)), res)
        out_hi[r, rows] = a / l

    _run_tiles([make_tile(2, n, hi_b.at[0, r], hi_b.at[1, r], hi_b.at[2, r], functools.partial(finish_hi, r))
                for r in range(d_hi) for n in range(S // d_hi // W_SUB)], lane_lo)

    for r in range(d_hi):
        r_lo, rows = hi_rows(r)
        out_lo[r_lo, rows, :] = out_hi[r]
    for r in range(d_lo):
        o_ref[pl.ds(r, S // d_lo, stride=d_lo), :] = out_lo[r].astype(o_ref.dtype)


def _dilated_attention(qkv, bias):
    _, B, HP, S, _ = qkv.shape
    d_lo, d_hi = DILATION_PAIRS[1][1], DILATION_PAIRS[2][1]
    nat = (3, S, LANES)
    by_lo = (d_lo, S // d_lo, LANES)
    by_hi = (d_hi, S // d_hi, LANES)
    qkv_spec = lambda g: pl.BlockSpec((None, None, None, S, LANES), lambda hp, b, g=g: (g, b, hp, 0, 0))
    return pl.pallas_call(
        _dilated_kernel,
        grid=(HP, B),
        in_specs=[qkv_spec(0), qkv_spec(1), qkv_spec(2),
                  pl.BlockSpec((None,) + bias.shape[1:], lambda hp, b: (hp, 0, 0, 0, 0))],
        out_specs=pl.BlockSpec((None, None, S, LANES), lambda hp, b: (b, hp, 0, 0)),
        out_shape=jax.ShapeDtypeStruct((B, HP, S, LANES), F32),
        scratch_shapes=[pltpu.VMEM(nat, F32), pltpu.VMEM((3,) + by_lo, F32), pltpu.VMEM((3,) + by_lo, BF16),
                        pltpu.VMEM((3,) + by_hi, BF16), pltpu.VMEM(nat, F32), pltpu.VMEM((3,) + by_lo, F32),
                        pltpu.VMEM((3,) + by_lo, F32), pltpu.VMEM((3,) + by_hi, F32),
                        pltpu.VMEM(by_hi, F32), pltpu.VMEM(by_lo, F32)],
        compiler_params=pltpu.CompilerParams(vmem_limit_bytes=VMEM_LIMIT),
        name="dilated_attn",
    )(qkv, qkv, qkv, bias)


def _moba_kernel(q_ref, k_ref, v_ref, bias_ref, o_ref, qa_s, ka_s):
    S = q_ref.shape[0]
    nb = S // MOBA_BLOCK
    n_cmp = nb * nb
    lane = lax.broadcasted_iota(jnp.int32, (1, LANES), 1)
    lane_lo = lane < HEAD_DIM

    kmean = jnp.sum(k_ref[...].astype(F32).reshape(nb, MOBA_BLOCK, LANES), axis=1) * (1.0 / MOBA_BLOCK)
    pad = jnp.zeros((LANES - n_cmp, LANES), F32)
    rhs_a = jnp.concatenate([kmean] * nb + [pad], axis=0).astype(BF16)
    rhs_b = jnp.concatenate([jnp.broadcast_to(kmean[i:i + 1], (nb, LANES)) for i in range(nb)] + [pad],
                            axis=0).astype(BF16)
    ci = lax.shift_right_logical(lane, int(math.log2(nb)))
    cj = lane & (nb - 1)
    tie = jnp.where(ci < cj, 1.0, 0.0)
    rr = lax.broadcasted_iota(jnp.int32, (LANES, LANES), 0)
    cc = lax.broadcasted_iota(jnp.int32, (LANES, LANES), 1)
    zero = jnp.zeros((), BF16)

    in_head = (lane_lo, jnp.logical_not(lane_lo))
    pen_base = tuple(HEAD_DIM * (1 - h) for h in range(HEADS_PER_SLAB))
    block_rows = lambda i: slice(i * MOBA_BLOCK, (i + 1) * MOBA_BLOCK)
    units = [(i, h) for i in range(nb) for h in range(HEADS_PER_SLAB)]
    gated = [(i, h) for i, h in units if i > MOBA_TOPK]

    gates = {}
    for h in range(HEADS_PER_SLAB):
        rhs_a_h = jnp.where(in_head[h], rhs_a, zero)
        rhs_b_h = jnp.where(in_head[h], rhs_b, zero)
        for i in range(nb):
            ka_s[h, block_rows(i)] = jnp.where(in_head[h], k_ref[block_rows(i)],
                                               jnp.where(lane - pen_base[h] == i, 1.0, 0.0).astype(BF16))
            if (i, h) in gated:
                gates[i, h] = (_nt_dot(q_ref[block_rows(i)], rhs_a_h), _nt_dot(q_ref[block_rows(i)], rhs_b_h))
            else:
                qa_s[h, block_rows(i)] = jnp.where(in_head[h], q_ref[block_rows(i)], zero)

    def finish_gating():
        ranks = {}
        for i, h in gated:
            ga, gb = gates.pop((i, h))
            past = jnp.where((ci < i) & (lane < n_cmp), 1.0, 0.0)
            beats = jnp.where(gb > ga, 1.0, jnp.where(gb == ga, tie, 0.0)) * past
            gather = jnp.where(((rr & (nb - 1)) == (cc - pen_base[h])) & (rr < n_cmp), 1.0, 0.0).astype(BF16)
            ranks[i, h] = _dot(beats.astype(BF16), gather)
        for i, h in gated:
            jl = lane - pen_base[h]
            pen_lanes = jnp.where((jl >= 0) & (jl < i), NEG_INF, 0.0)
            pen = jnp.where(ranks.pop((i, h)) < MOBA_TOPK - 0.5, 0.0, pen_lanes)
            qa_s[h, block_rows(i)] = jnp.where(in_head[h], q_ref[block_rows(i)], pen.astype(BF16))

    scores, outs = {}, {}
    for step in range(len(units) + MOBA_LOOKAHEAD):
        if step < len(units) and gates and units[step] in gated:
            finish_gating()
        done = step - MOBA_LOOKAHEAD
        if done >= 0:
            s = scores.pop(done)
            m = jnp.max(s, axis=-1, keepdims=True)
            p = jnp.exp(s - m)
            l = jnp.sum(p, axis=-1, keepdims=True)
            p = p.astype(BF16)
        if step < len(units):
            i, h = units[step]
            scores[step] = (_nt_dot(qa_s[h, block_rows(i)], ka_s[h, 0:(i + 1) * MOBA_BLOCK])
                            + bias_ref[h, :, (nb - 1 - i) * MOBA_BLOCK:])
        if done >= 0:
            i, h = units[done]
            outs[h] = _dot(p, v_ref[0:(i + 1) * MOBA_BLOCK]) / l
            if h == HEADS_PER_SLAB - 1:
                o_ref[block_rows(i)] = jnp.where(lane_lo, outs[0], outs[1]).astype(o_ref.dtype)


def _moba_attention(qkv, bias):
    _, B, HP, S, _ = qkv.shape
    qkv_spec = lambda g: pl.BlockSpec((None, None, None, S, LANES), lambda hp, b, g=g: (g, b, hp, 0, 0))
    return pl.pallas_call(
        _moba_kernel,
        grid=(HP, B),
        in_specs=[qkv_spec(3), qkv_spec(4), qkv_spec(5),
                  pl.BlockSpec((None,) + bias.shape[1:], lambda hp, b: (hp, 0, 0, 0))],
        out_specs=pl.BlockSpec((None, None, S, LANES), lambda hp, b: (b, hp, 0, 0)),
        out_shape=jax.ShapeDtypeStruct((B, HP, S, LANES), F32),
        scratch_shapes=[pltpu.VMEM((HEADS_PER_SLAB, S, LANES), BF16)] * 2,
        compiler_params=pltpu.CompilerParams(vmem_limit_bytes=VMEM_LIMIT),
        name="moba_attn",
    )(qkv, qkv, qkv, bias)


def _memkv_kernel(mem_ref, g_ref, w_ref, k_ref, v_ref):
    m = _rms(mem_ref[0], g_ref[...]).astype(BF16)
    kv = _dot(m, w_ref[...])
    d = k_ref.shape[-1]
    k_ref[0] = kv[:, :d].astype(BF16)
    v_ref[0] = kv[:, d:].astype(BF16)


def _memkv(mem, g, w_kv):
    B, M, D = mem.shape
    spec = pl.BlockSpec((1, M, D), lambda b: (b, 0, 0))
    return pl.pallas_call(
        _memkv_kernel,
        grid=(B,),
        in_specs=[spec, pl.BlockSpec(g.shape, lambda b: (0, 0)), pl.BlockSpec(w_kv.shape, lambda b: (0, 0))],
        out_specs=[spec, spec],
        out_shape=[jax.ShapeDtypeStruct((B, M, D), BF16)] * 2,
        compiler_params=pltpu.CompilerParams(vmem_limit_bytes=VMEM_LIMIT),
        name="memkv",
    )(mem, g, w_kv)


def _mix_kernel(ya_ref, yb_ref, x_ref, ga_ref, gb_ref, wout_ref, gc_ref, wq_ref, km_ref, vm_ref, wo_ref, o_ref):
    sub = x_ref.shape[1] // ROW_SPLIT
    d_head = x_ref.shape[2] // N_HEADS_MEM

    def part(k):
        rows = slice(k * sub, (k + 1) * sub)
        ya = jnp.concatenate([ya_ref[0, hp, rows] for hp in range(ya_ref.shape[1])], axis=-1)
        yb = jnp.concatenate([yb_ref[0, hp, rows] for hp in range(yb_ref.shape[1])], axis=-1)
        y = jnp.concatenate([_rms(ya, ga_ref[...]), _rms(yb, gb_ref[...])], axis=-1).astype(BF16)
        h1 = x_ref[0, rows] + _dot(y, wout_ref[...])
        yield
        qm = _dot(_rms(h1, gc_ref[...]).astype(BF16), wq_ref[...]).astype(BF16)
        yield
        heads = [slice(hd * d_head, (hd + 1) * d_head) for hd in range(N_HEADS_MEM)]
        scores = [_nt_dot(qm[:, cols], km_ref[0, :, cols]) for cols in heads]
        yield
        outs = []
        for s, cols in zip(scores, heads):
            m = jnp.max(s, axis=-1, keepdims=True)
            p = jnp.exp(s - m)
            l = jnp.sum(p, axis=-1, keepdims=True)
            outs.append((_dot(p.astype(BF16), vm_ref[0, :, cols]) / l).astype(BF16))
        yield
        o_ref[0, rows] = h1 + _dot(jnp.concatenate(outs, axis=-1), wo_ref[...])

    _interleave(part(k) for k in range(ROW_SPLIT))


def _mix(ya, yb, x, g_a, g_b, w_out, g_cross, w_q, km, vm, w_o):
    B, S, D = x.shape
    HP = ya.shape[1]
    M = km.shape[1]
    const = lambda a: pl.BlockSpec(a.shape, lambda b, t: (0,) * a.ndim, pipeline_mode=pl.Buffered(1))
    y_spec = pl.BlockSpec((1, HP, ROW_TILE, LANES), lambda b, t: (b, 0, t, 0))
    row_spec = pl.BlockSpec((1, ROW_TILE, D), lambda b, t: (b, t, 0))
    mem_spec = pl.BlockSpec((1, M, D), lambda b, t: (b, 0, 0))
    return pl.pallas_call(
        _mix_kernel,
        grid=(B, S // ROW_TILE),
        in_specs=[y_spec, y_spec, row_spec, const(g_a), const(g_b), const(w_out), const(g_cross), const(w_q),
                  mem_spec, mem_spec, const(w_o)],
        out_specs=row_spec,
        out_shape=jax.ShapeDtypeStruct((B, S, D), F32),
        compiler_params=pltpu.CompilerParams(vmem_limit_bytes=VMEM_LIMIT),
        name="mix",
    )(ya, yb, x, g_a, g_b, w_out, g_cross, w_q, km, vm, w_o)


def _ffn_kernel(h_ref, halo_ref, g_ref, wg_ref, wu_ref, cw_ref, cb_ref, wd_ref, gf_ref, o_ref, *, final_norm):
    sub = h_ref.shape[1] // FFN_ROW_SPLIT
    n_chunks = wg_ref.shape[1] // FFN_CHUNK
    chunk_cols = lambda c: slice(c * FFN_CHUNK, (c + 1) * FFN_CHUNK)

    def part(k):
        lo = k * sub
        if k == 0:
            prev = jnp.where(pl.program_id(1) > 0, halo_ref[0], 0.0)
        else:
            prev = h_ref[0, lo - HALO_ROWS:lo]
        h2 = h_ref[0, lo:lo + sub]
        f = _rms(jnp.concatenate([prev, h2], axis=0), g_ref[...]).astype(BF16)
        acc = jnp.zeros(h2.shape, F32)
        for c in range(n_chunks):
            cols = chunk_cols(c)
            a = _dot(f, wg_ref[:, cols])
            a_m1 = pltpu.roll(a, 1, 0)[HALO_ROWS:]
            a_m2 = pltpu.roll(a, 2, 0)[HALO_ROWS:]
            conv = cb_ref[:, cols] + a_m2 * cw_ref[0:1, cols]
            conv = conv + a_m1 * cw_ref[1:2, cols]
            conv = conv + a[HALO_ROWS:] * cw_ref[2:3, cols]
            up = _dot(f[HALO_ROWS:], wu_ref[:, cols])
            gated = (conv * jax.nn.sigmoid(conv) * up).astype(BF16)
            acc = acc + _dot(gated, wd_ref[cols, :])
            yield
        out = h2 + acc
        if final_norm:
            out = _rms(out, gf_ref[...])
        o_ref[0, lo:lo + sub] = out

    _interleave(part(k) for k in range(FFN_ROW_SPLIT))


def _ffn(h2, g_ffn, w_gate, w_up, conv_w, conv_b, w_down, g_final, final_norm):
    B, S, D = h2.shape
    assert HALO_ROWS >= CONV_WIDTH - 1 and CONV_WIDTH == 3
    const = lambda a: pl.BlockSpec(a.shape, lambda b, t: (0,) * a.ndim, pipeline_mode=pl.Buffered(1))
    row_spec = pl.BlockSpec((1, FFN_ROW_TILE, D), lambda b, t: (b, t, 0))
    halo_spec = pl.BlockSpec((1, HALO_ROWS, D),
                             lambda b, t: (b, jnp.maximum(t * (FFN_ROW_TILE // HALO_ROWS) - 1, 0), 0))
    return pl.pallas_call(
        functools.partial(_ffn_kernel, final_norm=final_norm),
        grid=(B, S // FFN_ROW_TILE),
        in_specs=[row_spec, halo_spec, const(g_ffn), const(w_gate), const(w_up), const(conv_w),
                  const(conv_b), const(w_down), const(g_final)],
        out_specs=row_spec,
        out_shape=jax.ShapeDtypeStruct((B, S, D), F32),
        compiler_params=pltpu.CompilerParams(vmem_limit_bytes=VMEM_LIMIT),
        name="conv_ffn",
    )(h2, h2, g_ffn, w_gate, w_up, conv_w, conv_b, w_down, g_final)


def _bias_lookup(table, bucket):
    onehot = (bucket.reshape(1, -1) == jnp.arange(N_BUCKETS)[:, None]).astype(F32)
    out = jnp.einsum("hk,kn->hn", table.astype(F32), onehot, precision=lax.Precision.HIGHEST)
    return out.reshape((table.shape[0],) + bucket.shape)


def _dilated_bias_tiles(table):
    i = jnp.arange(W_SUB)[:, None]
    j = jnp.arange(2 * W_SUB)[None, :]
    diff = W_SUB + i - j
    valid = (diff >= 0) & (diff <= W_SUB)
    buckets = jnp.stack([_rel_bucket(diff * d) for _, d in DILATION_PAIRS], axis=0)
    t = jnp.where(valid, _bias_lookup(table, buckets), NEG_INF)
    t = t.reshape(N_HEADS_DIL // HEADS_PER_SLAB, HEADS_PER_SLAB, len(DILATION_PAIRS), W_SUB, 2 * W_SUB)
    return t.transpose(0, 2, 1, 3, 4)


def _moba_bias_slabs(table, S):
    a = jnp.arange(MOBA_BLOCK)[:, None]
    c = jnp.arange(S)[None, :]
    dist = a - c + (S - MOBA_BLOCK)
    t = jnp.where(dist >= 0, _bias_lookup(table, _rel_bucket(dist)), NEG_INF)
    return t.reshape(N_HEADS_MOBA // HEADS_PER_SLAB, HEADS_PER_SLAB, MOBA_BLOCK, S)


def kernel(x, mem, w_in, g_mix, g_out_dil, g_out_moba, w_out, rel_bias, g_cross, g_mem, w_q_mem, w_kv_mem,
           w_o_mem, g_ffn, w_gate, w_up, conv_w, conv_b, w_down, g_final):
    B, S, D = x.shape
    depth = w_in.shape[0]
    dil_width = N_HEADS_DIL * HEAD_DIM
    mix_width = dil_width + N_HEADS_MOBA * HEAD_DIM
    assert S % (W_SUB * DILATION_PAIRS[-1][1]) == 0 and S % MOBA_BLOCK == 0 and S % ROW_TILE == 0
    assert all(w // d == W_SUB for w, d in DILATION_PAIRS) and w_in.shape[2] == 3 * mix_width

    dil_bias = _dilated_bias_tiles(rel_bias[:N_HEADS_DIL])
    moba_bias = _moba_bias_slabs(rel_bias[N_HEADS_DIL:], S)
    col = jnp.arange(3 * mix_width)
    moba_width = mix_width - dil_width
    is_q = (col < dil_width) | ((col >= 3 * dil_width) & (col < 3 * dil_width + moba_width))
    in_scale = jnp.where(is_q, HEAD_DIM ** -0.5, 1.0).astype(F32)
    row2 = lambda v: v.reshape(1, -1)

    h = x
    for l in range(depth):
        qkv = _inproj(h, row2(g_mix[l]), (w_in[l] * in_scale).astype(BF16))
        ya = _dilated_attention(qkv, dil_bias)
        yb = _moba_attention(qkv, moba_bias)
        w_q = (w_q_mem[l] * ((D // N_HEADS_MEM) ** -0.5)).astype(BF16)
        km, vm = _memkv(mem, row2(g_mem[l]), w_kv_mem[l].astype(BF16))
        h2 = _mix(ya, yb, h, row2(g_out_dil[l]), row2(g_out_moba[l]), w_out[l].astype(BF16), row2(g_cross[l]), w_q,
                  km, vm, w_o_mem[l].astype(BF16))
        h = _ffn(h2, row2(g_ffn[l]), w_gate[l].astype(BF16), w_up[l].astype(BF16), conv_w[l], row2(conv_b[l]),
                 w_down[l].astype(BF16), row2(g_final), final_norm=(l == depth - 1))
    return h
```

```python
import functools
import math

import jax
import jax.numpy as jnp
from jax import lax
from jax.experimental import pallas as pl
from jax.experimental.pallas import tpu as pltpu

F32 = jnp.float32
BF16 = jnp.bfloat16

HEAD_DIM = 64
LANES = 128
HEADS_PER_SLAB = LANES // HEAD_DIM
N_HEADS_DIL = 8
N_HEADS_MOBA = 8
DILATION_PAIRS = ((128, 1), (512, 4), (2048, 16))
W_SUB = 128
MOBA_BLOCK = 256
MOBA_TOPK = 3
N_BUCKETS = 32
MAX_DISTANCE = 2048
N_HEADS_MEM = 4
CONV_WIDTH = 3
EPS = 1e-6
NEG_INF = -1e30

ROW_TILE = 1024
ROW_SPLIT = 2
FFN_ROW_TILE = 512
FFN_ROW_SPLIT = 1
FFN_CHUNK = 256
SCORE_LOOKAHEAD = 3
MOBA_LOOKAHEAD = 1
HALO_ROWS = 16
VMEM_LIMIT = 56 * 1024 * 1024


def _rms(x, g):
    return x * lax.rsqrt(jnp.mean(x * x, axis=-1, keepdims=True) + EPS) * g


def _nt_dot(a, b):
    return lax.dot_general(a, b, (((1,), (1,)), ((), ())), preferred_element_type=F32)


def _dot(a, b):
    return jnp.dot(a, b, preferred_element_type=F32)


def _interleave(parts):
    parts = list(parts)
    while parts:
        for part in list(parts):
            try:
                next(part)
            except StopIteration:
                parts.remove(part)


def _rel_bucket(dist):
    max_exact = N_BUCKETS // 2
    n = jnp.maximum(dist, 0)
    nf = jnp.maximum(n, 1).astype(jnp.float32)
    large = max_exact + (jnp.log(nf / max_exact) / math.log(MAX_DISTANCE / max_exact)
                         * (N_BUCKETS - max_exact)).astype(jnp.int32)
    large = jnp.minimum(large, N_BUCKETS - 1)
    return jnp.where(n < max_exact, n, large)


def _inproj_kernel(x_ref, g_ref, w_ref, o_ref):
    sub = x_ref.shape[1] // ROW_SPLIT
    group = o_ref.shape[2] * LANES

    def part(k):
        rows = slice(k * sub, (k + 1) * sub)
        u = _rms(x_ref[0, rows], g_ref[...]).astype(BF16)
        for c in range(o_ref.shape[0]):
            res = _dot(u, w_ref[:, c * group:(c + 1) * group])
            yield
            for hp in range(o_ref.shape[2]):
                o_ref[c, 0, hp, rows] = res[:, hp * LANES:(hp + 1) * LANES].astype(BF16)

    _interleave(part(k) for k in range(ROW_SPLIT))


def _inproj(x, g, w):
    B, S, D = x.shape
    n_groups = w.shape[1] // (4 * LANES)
    return pl.pallas_call(
        _inproj_kernel,
        grid=(B, S // ROW_TILE),
        in_specs=[
            pl.BlockSpec((1, ROW_TILE, D), lambda b, t: (b, t, 0)),
            pl.BlockSpec((1, D), lambda b, t: (0, 0)),
            pl.BlockSpec(w.shape, lambda b, t: (0, 0), pipeline_mode=pl.Buffered(1)),
        ],
        out_specs=pl.BlockSpec((n_groups, 1, 4, ROW_TILE, LANES), lambda b, t: (0, b, 0, t, 0)),
        out_shape=jax.ShapeDtypeStruct((n_groups, B, 4, S, LANES), BF16),
        compiler_params=pltpu.CompilerParams(vmem_limit_bytes=VMEM_LIMIT),
        name="inproj",
    )(x, g, w)


def _score_stage(qt, kt, bias2, lane_lo):
    out = []
    for h in range(HEADS_PER_SLAB):
        in_head = lane_lo if h == 0 else jnp.logical_not(lane_lo)
        out.append(_nt_dot(jnp.where(in_head, qt, jnp.zeros_like(qt)), kt) + bias2[h])
    return out


def _softmax_stage(scores):
    out = []
    for s in scores:
        m = jnp.max(s, axis=-1, keepdims=True)
        p = jnp.exp(s - m)
        out.append((m, jnp.sum(p, axis=-1, keepdims=True), p.astype(BF16)))
    return out


def _value_stage(soft, vt, lane_lo):
    (m0, l0, p0), (m1, l1, p1) = soft
    return (jnp.where(lane_lo, m0, m1), jnp.where(lane_lo, l0, l1), jnp.where(lane_lo, _dot(p0, vt), _dot(p1, vt)))


def _run_tiles(tiles, lane_lo):
    scores = {}
    for step in range(len(tiles) + SCORE_LOOKAHEAD):
        done = step - SCORE_LOOKAHEAD
        if done >= 0:
            soft = _softmax_stage(scores.pop(done))
        if step < len(tiles):
            q, k, bias2 = tiles[step][0]()
            scores[step] = _score_stage(q, k, bias2, lane_lo)
        if done >= 0:
            tiles[done][2](_value_stage(soft, tiles[done][1](), lane_lo))


def _merge(old, new):
    m0, l0, a0 = old
    m1, l1, a1 = new
    m = jnp.maximum(m0, m1)
    e0 = jnp.exp(m0 - m)
    e1 = jnp.exp(m1 - m)
    return m, e0 * l0 + e1 * l1, e0 * a0 + e1 * a1


def _dilated_kernel(q_ref, k_ref, v_ref, bias_ref, o_ref, f_s, lo_s, lo_b, hi_b, nat_s, st0_s, st1_s, st2_s,
                    out_hi, out_lo):
    S = q_ref.shape[0]
    d_lo, d_hi = DILATION_PAIRS[1][1], DILATION_PAIRS[2][1]
    ratio = d_hi // d_lo
    assert DILATION_PAIRS[0][1] == 1 and d_hi == ratio * d_lo
    lane_lo = lax.broadcasted_iota(jnp.int32, (1, LANES), 1) < HEAD_DIM
    hi_rows = lambda r: (r % d_lo, pl.ds(r // d_lo, S // d_hi, stride=ratio))

    def make_tile(p, n, q_seq, k_seq, v_seq, finish):
        rows = slice(n * W_SUB, (n + 1) * W_SUB)
        if n == 0:
            keys = slice(0, W_SUB)
            bias2 = lambda: tuple(bias_ref[p, h, :, W_SUB:] for h in range(HEADS_PER_SLAB))
        else:
            keys = slice((n - 1) * W_SUB, (n + 1) * W_SUB)
            bias2 = lambda: tuple(bias_ref[p, h] for h in range(HEADS_PER_SLAB))
        return (lambda: (q_seq[rows], k_seq[keys], bias2()), lambda: v_seq[keys],
                functools.partial(finish, rows))

    for t, ref in enumerate((q_ref, k_ref, v_ref)):
        f_s[t] = ref[...].astype(F32)
        for r in range(d_lo):
            lo_s[t, r] = f_s[t, pl.ds(r, S // d_lo, stride=d_lo), :]
            lo_b[t, r] = lo_s[t, r].astype(BF16)
        for r in range(d_hi):
            r_lo, rows = hi_rows(r)
            hi_b[t, r] = lo_s[t, r_lo, rows, :].astype(BF16)

    def finish_d1(rows, res):
        for t in range(3):
            nat_s[t, rows] = res[t]

    _run_tiles([make_tile(0, n, q_ref, k_ref, v_ref, finish_d1) for n in range(S // W_SUB)], lane_lo)

    for t in range(3):
        for r in range(d_lo):
            st0_s[t, r] = nat_s[t, pl.ds(r, S // d_lo, stride=d_lo), :]

    def finish_lo(r, rows, res):
        merged = _merge(tuple(st0_s[t, r, rows] for t in range(3)), res)
        for t in range(3):
            st1_s[t, r, rows] = merged[t]

    _run_tiles([make_tile(1, n, lo_b.at[0, r], lo_b.at[1, r], lo_b.at[2, r], functools.partial(finish_lo, r))
                for r in range(d_lo) for n in range(S // d_lo // W_SUB)], lane_lo)

    for t in range(3):
        for r in range(d_hi):
            r_lo, rows = hi_rows(r)
            st2_s[t, r] = st1_s[t, r_lo, rows, :]

    def finish_hi(r, rows, res):
        _, l, a = _merge(tuple(st2_s[t, r, rows] for t in range(3)), res)
        out_hi[r, rows] = a / l

    _run_tiles([make_tile(2, n, hi_b.at[0, r], hi_b.at[1, r], hi_b.at[2, r], functools.partial(finish_hi, r))
                for r in range(d_hi) for n in range(S // d_hi // W_SUB)], lane_lo)

    for r in range(d_hi):
        r_lo, rows = hi_rows(r)
        out_lo[r_lo, rows, :] = out_hi[r]
    for r in range(d_lo):
        o_ref[pl.ds(r, S // d_lo, stride=d_lo), :] = out_lo[r].astype(o_ref.dtype)


def _dilated_attention(qkv, bias):
    _, B, HP, S, _ = qkv.shape
    d_lo, d_hi = DILATION_PAIRS[1][1], DILATION_PAIRS[2][1]
    nat = (3, S, LANES)
    by_lo = (d_lo, S // d_lo, LANES)
    by_hi = (d_hi, S // d_hi, LANES)
    qkv_spec = lambda g: pl.BlockSpec((None, None, None, S, LANES), lambda hp, b, g=g: (g, b, hp, 0, 0))
    return pl.pallas_call(
        _dilated_kernel,
        grid=(HP, B),
        in_specs=[qkv_spec(0), qkv_spec(1), qkv_spec(2),
                  pl.BlockSpec((None,) + bias.shape[1:], lambda hp, b: (hp, 0, 0, 0, 0))],
        out_specs=pl.BlockSpec((None, None, S, LANES), lambda hp, b: (b, hp, 0, 0)),
        out_shape=jax.ShapeDtypeStruct((B, HP, S, LANES), F32),
        scratch_shapes=[pltpu.VMEM(nat, F32), pltpu.VMEM((3,) + by_lo, F32), pltpu.VMEM((3,) + by_lo, BF16),
                        pltpu.VMEM((3,) + by_hi, BF16), pltpu.VMEM(nat, F32), pltpu.VMEM((3,) + by_lo, F32),
                        pltpu.VMEM((3,) + by_lo, F32), pltpu.VMEM((3,) + by_hi, F32),
                        pltpu.VMEM(by_hi, F32), pltpu.VMEM(by_lo, F32)],
        compiler_params=pltpu.CompilerParams(vmem_limit_bytes=VMEM_LIMIT),
        name="dilated_attn",
    )(qkv, qkv, qkv, bias)


def _moba_kernel(q_ref, k_ref, v_ref, bias_ref, o_ref, qa_s, ka_s):
    S = q_ref.shape[0]
    nb = S // MOBA_BLOCK
    n_cmp = nb * nb
    lane = lax.broadcasted_iota(jnp.int32, (1, LANES), 1)
    lane_lo = lane < HEAD_DIM

    kmean = jnp.sum(k_ref[...].astype(F32).reshape(nb, MOBA_BLOCK, LANES), axis=1) * (1.0 / MOBA_BLOCK)
    pad = jnp.zeros((LANES - n_cmp, LANES), F32)
    rhs_a = jnp.concatenate([kmean] * nb + [pad], axis=0).astype(BF16)
    rhs_b = jnp.concatenate([jnp.broadcast_to(kmean[i:i + 1], (nb, LANES)) for i in range(nb)] + [pad],
                            axis=0).astype(BF16)
    ci = lax.shift_right_logical(lane, int(math.log2(nb)))
    cj = lane & (nb - 1)
    tie = jnp.where(ci < cj, 1.0, 0.0)
    rr = lax.broadcasted_iota(jnp.int32, (LANES, LANES), 0)
    cc = lax.broadcasted_iota(jnp.int32, (LANES, LANES), 1)
    zero = jnp.zeros((), BF16)

    in_head = (lane_lo, jnp.logical_not(lane_lo))
    pen_base = tuple(HEAD_DIM * (1 - h) for h in range(HEADS_PER_SLAB))
    block_rows = lambda i: slice(i * MOBA_BLOCK, (i + 1) * MOBA_BLOCK)
    units = [(i, h) for i in range(nb) for h in range(HEADS_PER_SLAB)]
    gated = [(i, h) for i, h in units if i > MOBA_TOPK]

    gates = {}
    for h in range(HEADS_PER_SLAB):
        rhs_a_h = jnp.where(in_head[h], rhs_a, zero)
        rhs_b_h = jnp.where(in_head[h], rhs_b, zero)
        for i in range(nb):
            ka_s[h, block_rows(i)] = jnp.where(in_head[h], k_ref[block_rows(i)],
                                               jnp.where(lane - pen_base[h] == i, 1.0, 0.0).astype(BF16))
            if (i, h) in gated:
                gates[i, h] = (_nt_dot(q_ref[block_rows(i)], rhs_a_h), _nt_dot(q_ref[block_rows(i)], rhs_b_h))
            else:
                qa_s[h, block_rows(i)] = jnp.where(in_head[h], q_ref[block_rows(i)], zero)

    def finish_gating():
        ranks = {}
        for i, h in gated:
            ga, gb = gates.pop((i, h))
            past = jnp.where((ci < i) & (lane < n_cmp), 1.0, 0.0)
            beats = jnp.where(gb > ga, 1.0, jnp.where(gb == ga, tie, 0.0)) * past
            gather = jnp.where(((rr & (nb - 1)) == (cc - pen_base[h])) & (rr < n_cmp), 1.0, 0.0).astype(BF16)
            ranks[i, h] = _dot(beats.astype(BF16), gather)
        for i, h in gated:
            jl = lane - pen_base[h]
            pen_lanes = jnp.where((jl >= 0) & (jl < i), NEG_INF, 0.0)
            pen = jnp.where(ranks.pop((i, h)) < MOBA_TOPK - 0.5, 0.0, pen_lanes)
            qa_s[h, block_rows(i)] = jnp.where(in_head[h], q_ref[block_rows(i)], pen.astype(BF16))

    scores, outs = {}, {}
    for step in range(len(units) + MOBA_LOOKAHEAD):
        if step < len(units) and gates and units[step] in gated:
            finish_gating()
        done = step - MOBA_LOOKAHEAD
        if done >= 0:
            s = scores.pop(done)
            m = jnp.max(s, axis=-1, keepdims=True)
            p = jnp.exp(s - m)
            l = jnp.sum(p, axis=-1, keepdims=True)
            p = p.astype(BF16)
        if step < len(units):
            i, h = units[step]
            scores[step] = (_nt_dot(qa_s[h, block_rows(i)], ka_s[h, 0:(i + 1) * MOBA_BLOCK])
                            + bias_ref[h, :, (nb - 1 - i) * MOBA_BLOCK:])
        if done >= 0:
            i, h = units[done]
            outs[h] = _dot(p, v_ref[0:(i + 1) * MOBA_BLOCK]) / l
            if h == HEADS_PER_SLAB - 1:
                o_ref[block_rows(i)] = jnp.where(lane_lo, outs[0], outs[1]).astype(o_ref.dtype)


def _moba_attention(qkv, bias):
    _, B, HP, S, _ = qkv.shape
    qkv_spec = lambda g: pl.BlockSpec((None, None, None, S, LANES), lambda hp, b, g=g: (g, b, hp, 0, 0))
    return pl.pallas_call(
        _moba_kernel,
        grid=(HP, B),
        in_specs=[qkv_spec(3), qkv_spec(4), qkv_spec(5),
                  pl.BlockSpec((None,) + bias.shape[1:], lambda hp, b: (hp, 0, 0, 0))],
        out_specs=pl.BlockSpec((None, None, S, LANES), lambda hp, b: (b, hp, 0, 0)),
        out_shape=jax.ShapeDtypeStruct((B, HP, S, LANES), F32),
        scratch_shapes=[pltpu.VMEM((HEADS_PER_SLAB, S, LANES), BF16)] * 2,
        compiler_params=pltpu.CompilerParams(vmem_limit_bytes=VMEM_LIMIT),
        name="moba_attn",
    )(qkv, qkv, qkv, bias)


def _memkv_kernel(mem_ref, g_ref, w_ref, k_ref, v_ref):
    m = _rms(mem_ref[0], g_ref[...]).astype(BF16)
    kv = _dot(m, w_ref[...])
    d = k_ref.shape[-1]
    k_ref[0] = kv[:, :d].astype(BF16)
    v_ref[0] = kv[:, d:].astype(BF16)


def _memkv(mem, g, w_kv):
    B, M, D = mem.shape
    spec = pl.BlockSpec((1, M, D), lambda b: (b, 0, 0))
    return pl.pallas_call(
        _memkv_kernel,
        grid=(B,),
        in_specs=[spec, pl.BlockSpec(g.shape, lambda b: (0, 0)), pl.BlockSpec(w_kv.shape, lambda b: (0, 0))],
        out_specs=[spec, spec],
        out_shape=[jax.ShapeDtypeStruct((B, M, D), BF16)] * 2,
        compiler_params=pltpu.CompilerParams(vmem_limit_bytes=VMEM_LIMIT),
        name="memkv",
    )(mem, g, w_kv)


def _mix_kernel(ya_ref, yb_ref, x_ref, ga_ref, gb_ref, wout_ref, gc_ref, wq_ref, mem_ref, gm_ref, wkv_ref, wo_ref,
                o_ref, km_s, vm_s):
    sub = x_ref.shape[1] // ROW_SPLIT
    d_model = x_ref.shape[2]
    d_head = d_model // N_HEADS_MEM

    @pl.when(pl.program_id(1) == 0)
    def _():
        kv = _dot(_rms(mem_ref[0], gm_ref[...]).astype(BF16), wkv_ref[...])
        km_s[...] = kv[:, :d_model].astype(BF16)
        vm_s[...] = kv[:, d_model:].astype(BF16)

    def part(k):
        rows = slice(k * sub, (k + 1) * sub)
        ya = jnp.concatenate([ya_ref[0, hp, rows] for hp in range(ya_ref.shape[1])], axis=-1)
        yb = jnp.concatenate([yb_ref[0, hp, rows] for hp in range(yb_ref.shape[1])], axis=-1)
        y = jnp.concatenate([_rms(ya, ga_ref[...]), _rms(yb, gb_ref[...])], axis=-1).astype(BF16)
        h1 = x_ref[0, rows] + _dot(y, wout_ref[...])
        yield
        qm = _dot(_rms(h1, gc_ref[...]).astype(BF16), wq_ref[...]).astype(BF16)
        yield
        heads = [slice(hd * d_head, (hd + 1) * d_head) for hd in range(N_HEADS_MEM)]
        scores = [_nt_dot(qm[:, cols], km_s[:, cols]) for cols in heads]
        yield
        outs = []
        for s, cols in zip(scores, heads):
            m = jnp.max(s, axis=-1, keepdims=True)
            p = jnp.exp(s - m)
            l = jnp.sum(p, axis=-1, keepdims=True)
            outs.append((_dot(p.astype(BF16), vm_s[:, cols]) / l).astype(BF16))
        yield
        o_ref[0, rows] = h1 + _dot(jnp.concatenate(outs, axis=-1), wo_ref[...])

    _interleave(part(k) for k in range(ROW_SPLIT))


def _mix(ya, yb, x, g_a, g_b, w_out, g_cross, w_q, mem, g_mem, w_kv, w_o):
    B, S, D = x.shape
    HP = ya.shape[1]
    M = mem.shape[1]
    const = lambda a: pl.BlockSpec(a.shape, lambda b, t: (0,) * a.ndim, pipeline_mode=pl.Buffered(1))
    y_spec = pl.BlockSpec((1, HP, ROW_TILE, LANES), lambda b, t: (b, 0, t, 0))
    row_spec = pl.BlockSpec((1, ROW_TILE, D), lambda b, t: (b, t, 0))
    mem_spec = pl.BlockSpec((1, M, D), lambda b, t: (b, 0, 0))
    return pl.pallas_call(
        _mix_kernel,
        grid=(B, S // ROW_TILE),
        in_specs=[y_spec, y_spec, row_spec, const(g_a), const(g_b), const(w_out), const(g_cross), const(w_q),
                  mem_spec, const(g_mem), const(w_kv), const(w_o)],
        out_specs=row_spec,
        out_shape=jax.ShapeDtypeStruct((B, S, D), F32),
        scratch_shapes=[pltpu.VMEM((M, D), BF16), pltpu.VMEM((M, D), BF16)],
        compiler_params=pltpu.CompilerParams(vmem_limit_bytes=VMEM_LIMIT),
        name="mix",
    )(ya, yb, x, g_a, g_b, w_out, g_cross, w_q, mem, g_mem, w_kv, w_o)


def _ffn_kernel(h_ref, halo_ref, g_ref, wg_ref, wu_ref, cw_ref, cb_ref, wd_ref, gf_ref, o_ref, *, final_norm):
    sub = h_ref.shape[1] // FFN_ROW_SPLIT
    n_chunks = wg_ref.shape[1] // FFN_CHUNK
    chunk_cols = lambda c: slice(c * FFN_CHUNK, (c + 1) * FFN_CHUNK)

    def part(k):
        lo = k * sub
        if k == 0:
            prev = jnp.where(pl.program_id(1) > 0, halo_ref[0], 0.0)
        else:
            prev = h_ref[0, lo - HALO_ROWS:lo]
        h2 = h_ref[0, lo:lo + sub]
        f = _rms(jnp.concatenate([prev, h2], axis=0), g_ref[...]).astype(BF16)
        acc = jnp.zeros(h2.shape, F32)
        for c in range(n_chunks):
            cols = chunk_cols(c)
            a = _dot(f, wg_ref[:, cols])
            a_m1 = pltpu.roll(a, 1, 0)[HALO_ROWS:]
            a_m2 = pltpu.roll(a, 2, 0)[HALO_ROWS:]
            conv = cb_ref[:, cols] + a_m2 * cw_ref[0:1, cols]
            conv = conv + a_m1 * cw_ref[1:2, cols]
            conv = conv + a[HALO_ROWS:] * cw_ref[2:3, cols]
            up = _dot(f[HALO_ROWS:], wu_ref[:, cols])
            gated = (conv * jax.nn.sigmoid(conv) * up).astype(BF16)
            acc = acc + _dot(gated, wd_ref[cols, :])
            yield
        out = h2 + acc
        if final_norm:
            out = _rms(out, gf_ref[...])
        o_ref[0, lo:lo + sub] = out

    _interleave(part(k) for k in range(FFN_ROW_SPLIT))


def _ffn(h2, g_ffn, w_gate, w_up, conv_w, conv_b, w_down, g_final, final_norm):
    B, S, D = h2.shape
    assert HALO_ROWS >= CONV_WIDTH - 1 and CONV_WIDTH == 3
    const = lambda a: pl.BlockSpec(a.shape, lambda b, t: (0,) * a.ndim, pipeline_mode=pl.Buffered(1))
    row_spec = pl.BlockSpec((1, FFN_ROW_TILE, D), lambda b, t: (b, t, 0))
    halo_spec = pl.BlockSpec((1, HALO_ROWS, D),
                             lambda b, t: (b, jnp.maximum(t * (FFN_ROW_TILE // HALO_ROWS) - 1, 0), 0))
    return pl.pallas_call(
        functools.partial(_ffn_kernel, final_norm=final_norm),
        grid=(B, S // FFN_ROW_TILE),
        in_specs=[row_spec, halo_spec, const(g_ffn), const(w_gate), const(w_up), const(conv_w),
                  const(conv_b), const(w_down), const(g_final)],
        out_specs=row_spec,
        out_shape=jax.ShapeDtypeStruct((B, S, D), F32),
        compiler_params=pltpu.CompilerParams(vmem_limit_bytes=VMEM_LIMIT),
        name="conv_ffn",
    )(h2, h2, g_ffn, w_gate, w_up, conv_w, conv_b, w_down, g_final)


def _bias_lookup(table, bucket):
    onehot = (bucket.reshape(1, -1) == jnp.arange(N_BUCKETS)[:, None]).astype(F32)
    out = jnp.einsum("hk,kn->hn", table.astype(F32), onehot, precision=lax.Precision.HIGHEST)
    return out.reshape((table.shape[0],) + bucket.shape)


def _dilated_bias_tiles(table):
    i = jnp.arange(W_SUB)[:, None]
    j = jnp.arange(2 * W_SUB)[None, :]
    diff = W_SUB + i - j
    valid = (diff >= 0) & (diff <= W_SUB)
    buckets = jnp.stack([_rel_bucket(diff * d) for _, d in DILATION_PAIRS], axis=0)
    t = jnp.where(valid, _bias_lookup(table, buckets), NEG_INF)
    t = t.reshape(N_HEADS_DIL // HEADS_PER_SLAB, HEADS_PER_SLAB, len(DILATION_PAIRS), W_SUB, 2 * W_SUB)
    return t.transpose(0, 2, 1, 3, 4)


def _moba_bias_slabs(table, S):
    a = jnp.arange(MOBA_BLOCK)[:, None]
    c = jnp.arange(S)[None, :]
    dist = a - c + (S - MOBA_BLOCK)
    t = jnp.where(dist >= 0, _bias_lookup(table, _rel_bucket(dist)), NEG_INF)
    return t.reshape(N_HEADS_MOBA // HEADS_PER_SLAB, HEADS_PER_SLAB, MOBA_BLOCK, S)


def kernel(x, mem, w_in, g_mix, g_out_dil, g_out_moba, w_out, rel_bias, g_cross, g_mem, w_q_mem, w_kv_mem,
           w_o_mem, g_ffn, w_gate, w_up, conv_w, conv_b, w_down, g_final):
    B, S, D = x.shape
    depth = w_in.shape[0]
    dil_width = N_HEADS_DIL * HEAD_DIM
    mix_width = dil_width + N_HEADS_MOBA * HEAD_DIM
    assert S % (W_SUB * DILATION_PAIRS[-1][1]) == 0 and S % MOBA_BLOCK == 0 and S % ROW_TILE == 0
    assert all(w // d == W_SUB for w, d in DILATION_PAIRS) and w_in.shape[2] == 3 * mix_width

    dil_bias = _dilated_bias_tiles(rel_bias[:N_HEADS_DIL])
    moba_bias = _moba_bias_slabs(rel_bias[N_HEADS_DIL:], S)
    col = jnp.arange(3 * mix_width)
    moba_width = mix_width - dil_width
    is_q = (col < dil_width) | ((col >= 3 * dil_width) & (col < 3 * dil_width + moba_width))
    in_scale = jnp.where(is_q, HEAD_DIM ** -0.5, 1.0).astype(F32)
    row2 = lambda v: v.reshape(1, -1)

    h = x
    for l in range(depth):
        qkv = _inproj(h, row2(g_mix[l]), (w_in[l] * in_scale).astype(BF16))
        ya = _dilated_attention(qkv, dil_bias)
        yb = _moba_attention(qkv, moba_bias)
        w_q = (w_q_mem[l] * ((D // N_HEADS_MEM) ** -0.5)).astype(BF16)
        h2 = _mix(ya, yb, h, row2(g_out_dil[l]), row2(g_out_moba[l]), w_out[l].astype(BF16), row2(g_cross[l]), w_q,
                  mem, row2(g_mem[l]), w_kv_mem[l].astype(BF16), w_o_mem[l].astype(BF16))
        h = _ffn(h2, row2(g_ffn[l]), w_gate[l].astype(BF16), w_up[l].astype(BF16), conv_w[l], row2(conv_b[l]),
                 w_down[l].astype(BF16), row2(g_final), final_norm=(l == depth - 1))
    return h
```
